```python
import jax, jax.numpy as jnp
from jax import lax
import numpy as np

D_MODEL = 1024
BATCH = 16
SEQ = 2048
DEPTH = 2
DEC_BATCH = 32
DEC_SEQ = 4
PAST_LEN = 16384
PAGE_SIZE = 128

N_A_LAYERS = DEPTH // 2
N_B_LAYERS = DEPTH - N_A_LAYERS
H_A = 4
DK_A = D_MODEL // H_A
DV_A = D_MODEL // H_A
H_B = 8
DH_B = D_MODEL // H_B
D_FF = -(-8 * D_MODEL // (3 * 256)) * 256
CHUNK = 128
Q_BLOCK = 128
RMS_EPS = 1e-6
GN_EPS = 1e-5
ROPE_BASE = 10000.0
FORGET_BIAS = 5.0

kernel_name = 'yoco_retnet_fox_step'


def rmsnorm(x, g):
    xf = x.astype(jnp.float32)
    y = xf * lax.rsqrt(jnp.mean(xf * xf, axis=-1, keepdims=True) + RMS_EPS)
    return (y * g.astype(jnp.float32)).astype(x.dtype)


def swiglu(x, w_up, w_down):
    g, u = jnp.split(x @ w_up, 2, axis=-1)
    return (jax.nn.silu(g) * u) @ w_down


def rotary(x, pos):
    half = x.shape[-1] // 2
    inv = 1.0 / (ROPE_BASE ** jnp.linspace(0.0, 1.0, half, dtype=jnp.float32))
    ang = pos.astype(jnp.float32)[:, None] * inv[None, :]
    cos = jnp.cos(ang)[None, :, None, :]
    sin = jnp.sin(ang)[None, :, None, :]
    x1 = x[..., 0::2]
    x2 = x[..., 1::2]
    return jnp.stack([x1 * cos - x2 * sin, x1 * sin + x2 * cos], axis=-1).reshape(x.shape)


def log_decay():
    return jnp.log(1.0 - jnp.exp2(-5.0 - jnp.arange(H_A, dtype=jnp.float32)))


def retention_chunkwise(q, k, v, s0, chunk):
    B, T, H, DK = q.shape
    DV = v.shape[-1]
    n = T // chunk
    lg = log_decay()
    i = jnp.arange(chunk, dtype=jnp.float32)
    diff = i[:, None] - i[None, :]
    intra = jnp.where(diff >= 0, jnp.exp(lg[:, None, None] * jnp.maximum(diff, 0.0)), 0.0)
    q_dec = jnp.exp(lg[:, None] * (i[None, :] + 1.0))
    k_dec = jnp.exp(lg[:, None] * (chunk - 1.0 - i[None, :]))
    c_dec = jnp.exp(lg * chunk)
    qc = q.reshape(B, n, chunk, H, DK).transpose(1, 0, 3, 2, 4)
    kc = k.reshape(B, n, chunk, H, DK).transpose(1, 0, 3, 2, 4)
    vc = v.reshape(B, n, chunk, H, DV).transpose(1, 0, 3, 2, 4)

    def step(s, inp):
        qb, kb, vb = inp
        scores = jnp.einsum('bhid,bhjd->bhij', qb, kb) * intra[None]
        o = (jnp.einsum('bhij,bhjv->bhiv', scores, vb)
             + jnp.einsum('bhid,bhdv->bhiv', qb, s) * q_dec[None, :, :, None])
        s_new = (s * c_dec[None, :, None, None]
                 + jnp.einsum('bhjd,bhjv->bhdv', kb * k_dec[None, :, :, None], vb))
        return s_new, o

    s_fin, o = lax.scan(step, s0, (qc, kc, vc))
    o = o.transpose(1, 0, 3, 2, 4).reshape(B, T, H, DV)
    return o, s_fin


def retention_layer(hn, w_in, gn_g, w_out, s0, pos, chunk):
    B, T, _ = hn.shape
    q, k, v, g = jnp.split(hn @ w_in, 4, axis=-1)
    q = rotary(q.reshape(B, T, H_A, DK_A).astype(jnp.float32), pos)
    k = rotary(k.reshape(B, T, H_A, DK_A).astype(jnp.float32), pos) * (DK_A ** -0.5)
    v = v.reshape(B, T, H_A, DV_A).astype(jnp.float32)
    o, s = retention_chunkwise(q, k, v, s0.astype(jnp.float32), chunk)
    mu = jnp.mean(o, axis=-1, keepdims=True)
    var = jnp.mean(jnp.square(o - mu), axis=-1, keepdims=True)
    o = ((o - mu) * lax.rsqrt(var + GN_EPS)).reshape(B, T, H_A * DV_A) * gn_g.astype(jnp.float32)
    y = (jax.nn.silu(g.astype(jnp.float32)) * o).astype(hn.dtype) @ w_out
    return y, s.astype(s0.dtype)


def shared_kv(h, norm_kv, w_kvf, b_f):
    B, T, _ = h.shape
    proj = rmsnorm(h, norm_kv) @ w_kvf
    k = proj[..., :D_MODEL].reshape(B, T, H_B, DH_B)
    v = proj[..., D_MODEL:2 * D_MODEL].reshape(B, T, H_B, DH_B)
    logf = jax.nn.log_sigmoid(proj[..., 2 * D_MODEL:].astype(jnp.float32) + b_f.astype(jnp.float32))
    return k, v, logf


def fox_prompt(q, k, v, logf):
    B, S, H, Dh = q.shape
    nb = S // Q_BLOCK
    c = jnp.cumsum(logf, axis=1).transpose(0, 2, 1)
    qb = q.reshape(B, nb, Q_BLOCK, H, Dh).transpose(1, 0, 2, 3, 4)
    cq = c.reshape(B, H, nb, Q_BLOCK).transpose(2, 0, 1, 3)
    kpos = jnp.arange(S)
    scale = DH_B ** -0.5

    def block(args):
        qi, ci, bi = args
        s = jnp.einsum('bqhd,bkhd->bhqk', qi, k).astype(jnp.float32) * scale
        s = s + ci[..., None] - c[:, :, None, :]
        qpos = bi * Q_BLOCK + jnp.arange(Q_BLOCK)
        s = jnp.where((kpos[None, :] <= qpos[:, None])[None, None], s, -jnp.inf)
        p = jax.nn.softmax(s, axis=-1)
        return jnp.einsum('bhqk,bkhd->bqhd', p.astype(v.dtype), v)

    o = lax.map(block, (qb, cq, jnp.arange(nb)))
    return o.transpose(1, 0, 2, 3, 4).reshape(B, S, H, Dh)


def fox_sample(q, k_new, v_new, logf_new, cache_k, cache_v, cache_logf, page_table):
    Bd, T, H, Dh = q.shape
    kp = cache_k[page_table].reshape(Bd, -1, H, Dh)
    vp = cache_v[page_table].reshape(Bd, -1, H, Dh)
    lp = cache_logf[page_table].reshape(Bd, -1, H)
    P = kp.shape[1]
    k = jnp.concatenate([kp, k_new.astype(kp.dtype)], axis=1)
    v = jnp.concatenate([vp, v_new.astype(vp.dtype)], axis=1)
    l = jnp.concatenate([lp.astype(jnp.float32), logf_new], axis=1)
    c = jnp.cumsum(l, axis=1).transpose(0, 2, 1)
    s = jnp.einsum('bqhd,bkhd->bhqk', q.astype(k.dtype), k).astype(jnp.float32) * (DH_B ** -0.5)
    s = s + c[:, :, P:, None] - c[:, :, None, :]
    kpos = jnp.arange(P + T)
    qpos = P + jnp.arange(T)
    s = jnp.where((kpos[None, :] <= qpos[:, None])[None, None], s, -jnp.inf)
    p = jax.nn.softmax(s, axis=-1)
    return jnp.einsum('bhqk,bkhd->bqhd', p.astype(v.dtype), v).astype(q.dtype)


def fox_layer(hn, w_qg, w_o, kv, attend):
    B, T, _ = hn.shape
    q, gate = jnp.split(hn @ w_qg, 2, axis=-1)
    o = attend(q.reshape(B, T, H_B, DH_B), kv).reshape(B, T, D_MODEL)
    return (jax.nn.sigmoid(gate) * o) @ w_o


def trunk(x, ret_s0, pos, chunk, attend, norm_mix, w_ret_in, gn_ret, w_ret_out, norm_kv, w_kvf, b_f,
          w_b_qg, w_b_out, norm_ffn, w_ffn_up, w_ffn_down, norm_final):
    h = x
    ret_states = []
    kv = None
    for layer in range(DEPTH):
        if layer == N_A_LAYERS:
            kv = shared_kv(h, norm_kv, w_kvf, b_f)
        hn = rmsnorm(h, norm_mix[layer])
        if layer < N_A_LAYERS:
            y, s = retention_layer(hn, w_ret_in[layer], gn_ret[layer], w_ret_out[layer], ret_s0[layer], pos, chunk)
            ret_states.append(s)
        else:
            j = layer - N_A_LAYERS
            y = fox_layer(hn, w_b_qg[j], w_b_out[j], kv, attend)
        h = h + y
        h = h + swiglu(rmsnorm(h, norm_ffn[layer]), w_ffn_up[layer], w_ffn_down[layer])
    return rmsnorm(h, norm_final), jnp.stack(ret_states), kv


def setup_inputs(seed: int = 0) -> dict:
    key = jax.random.key(seed)
    ks = jax.random.split(key, 24)
    f32 = jnp.float32
    n_pages = PAST_LEN // PAGE_SIZE
    n_used = DEC_BATCH * n_pages
    n_pool = n_used + (n_used + 3) // 4

    def nrm(k, shape, scale):
        return jax.random.normal(k, shape, f32) * scale

    page_table = jax.random.permutation(ks[0], n_pool)[:n_used].reshape(DEC_BATCH, n_pages).astype(jnp.int32)
    w_kvf = jnp.concatenate([nrm(ks[1], (D_MODEL, 2 * D_MODEL), D_MODEL ** -0.5),
                             nrm(ks[2], (D_MODEL, H_B), 0.1 * D_MODEL ** -0.5)], axis=1)
    return {
        'x_prompt': nrm(ks[3], (BATCH, SEQ, D_MODEL), 1.0),
        'x_sample': nrm(ks[4], (DEC_BATCH, DEC_SEQ, D_MODEL), 1.0),
        'state_ret': nrm(ks[5], (N_A_LAYERS, DEC_BATCH, H_A, DK_A, DV_A), 0.5),
        'cache_k': nrm(ks[6], (n_pool, PAGE_SIZE, H_B, DH_B), 1.0),
        'cache_v': nrm(ks[7], (n_pool, PAGE_SIZE, H_B, DH_B), 1.0),
        'cache_logf': jax.nn.log_sigmoid(FORGET_BIAS + nrm(ks[8], (n_pool, PAGE_SIZE, H_B), 0.5)),
        'page_table': page_table,
        'norm_mix': 1.0 + nrm(ks[9], (DEPTH, D_MODEL), 0.02),
        'w_ret_in': nrm(ks[10], (N_A_LAYERS, D_MODEL, 4 * D_MODEL), D_MODEL ** -0.5),
        'gn_ret': 1.0 + nrm(ks[11], (N_A_LAYERS, H_A * DV_A), 0.02),
        'w_ret_out': nrm(ks[12], (N_A_LAYERS, H_A * DV_A, D_MODEL), D_MODEL ** -0.5),
        'norm_kv': 1.0 + nrm(ks[13], (D_MODEL,), 0.02),
        'w_kvf': w_kvf,
        'b_f': FORGET_BIAS + nrm(ks[14], (H_B,), 0.1),
        'w_b_qg': nrm(ks[15], (N_B_LAYERS, D_MODEL, 2 * D_MODEL), D_MODEL ** -0.5),
        'w_b_out': nrm(ks[16], (N_B_LAYERS, D_MODEL, D_MODEL), D_MODEL ** -0.5),
        'norm_ffn': 1.0 + nrm(ks[17], (DEPTH, D_MODEL), 0.02),
        'w_ffn_up': nrm(ks[18], (DEPTH, D_MODEL, 2 * D_FF), D_MODEL ** -0.5),
        'w_ffn_down': nrm(ks[19], (DEPTH, D_FF, D_MODEL), D_FF ** -0.5),
        'norm_final': 1.0 + nrm(ks[20], (D_MODEL,), 0.02),
    }


def reference(x_prompt, x_sample, state_ret, cache_k, cache_v, cache_logf, page_table,
              norm_mix, w_ret_in, gn_ret, w_ret_out, norm_kv, w_kvf, b_f,
              w_b_qg, w_b_out, norm_ffn, w_ffn_up, w_ffn_down, norm_final):
    weights = (norm_mix, w_ret_in, gn_ret, w_ret_out, norm_kv, w_kvf, b_f,
               w_b_qg, w_b_out, norm_ffn, w_ffn_up, w_ffn_down, norm_final)
    b, t = x_prompt.shape[0], x_prompt.shape[1]
    s0_prompt = jnp.zeros((N_A_LAYERS, b, H_A, DK_A, DV_A), jnp.float32)
    y_prompt, ret_state_prompt, kv_p = trunk(
        x_prompt, s0_prompt, jnp.arange(t), min(CHUNK, t),
        lambda q, kv: fox_prompt(q, kv[0], kv[1], kv[2]), *weights)
    td = x_sample.shape[1]
    past = page_table.shape[1] * cache_k.shape[1]
    y_sample, ret_state_sample, kv_s = trunk(
        x_sample, state_ret, past + jnp.arange(td), td,
        lambda q, kv: fox_sample(q, kv[0], kv[1], kv[2], cache_k, cache_v, cache_logf, page_table), *weights)
    k_prompt, v_prompt, logf_prompt = kv_p
    k_sample, v_sample, logf_sample = kv_s
    return (y_prompt, y_sample, ret_state_prompt, k_prompt, v_prompt, logf_prompt,
            ret_state_sample, k_sample, v_sample, logf_sample)
```

```python
import functools

import jax
import jax.numpy as jnp
from jax import lax
from jax.experimental import pallas as pl
from jax.experimental.pallas import tpu as pltpu

H_A = 4
H_B = 8
RMS_EPS = 1e-6
GN_EPS = 1e-5
ROPE_BASE = 10000.0
LANES = 128
ROWS_BF16 = 16
VMEM_LIMIT_BYTES = 56 * 1024 * 1024

F32 = jnp.float32
BF16 = jnp.bfloat16


def _dot(a, b):
    return jnp.dot(a, b, preferred_element_type=F32)


def _dot_nt(a, b):
    return lax.dot_general(a, b, (((1,), (1,)), ((), ())), preferred_element_type=F32)


def _dot_tn(a, b):
    return lax.dot_general(a, b, (((0,), (0,)), ((), ())), preferred_element_type=F32)


def _rmsnorm(x, g):
    return x * lax.rsqrt(jnp.mean(x * x, axis=-1, keepdims=True) + RMS_EPS) * g


def _sigmoid(x):
    return 1.0 / (1.0 + jnp.exp(-x))


def _log_sigmoid(x):
    return jnp.minimum(x, 0.0) - jnp.log1p(jnp.exp(-jnp.abs(x)))


def _split3(x):
    hi = x.astype(BF16)
    r1 = x - hi.astype(F32)
    mid = r1.astype(BF16)
    lo = (r1 - mid.astype(F32)).astype(BF16)
    return hi, mid, lo


def _const_spec(shape):
    nd = len(shape)
    return pl.BlockSpec(shape, lambda *_: (0,) * nd, pipeline_mode=pl.Buffered(1))


def _params(*sem):
    return pltpu.CompilerParams(dimension_semantics=sem, vmem_limit_bytes=VMEM_LIMIT_BYTES)


def _ret_kernel(x_ref, s0_ref, cos_ref, sin_ref, g_ref, win_ref, gn_ref, wout_ref,
                intra_ref, qdec_ref, kdec_ref, cdec_ref,
                h_ref, sfin_ref, s_scr, gate_scr, *, chunk):
    t = pl.program_id(1)
    tc, d = x_ref.shape[1], x_ref.shape[2]
    dk = d // H_A
    half = dk // 2

    nblk = dk // LANES

    def flat_rows(parity, c):
        return pl.ds(parity * nblk + c, half, stride=2 * nblk)

    @pl.when(t == 0)
    def _():
        for h in range(H_A):
            for parity in range(2):
                for c in range(nblk):
                    s_scr[h, parity * half:(parity + 1) * half, c * LANES:(c + 1) * LANES] = (
                        s0_ref[0, h, flat_rows(parity, c), :])

    x = x_ref[0]
    hn = _rmsnorm(x, g_ref[...]).astype(BF16)
    cos = cos_ref[...]
    sin = sin_ref[...]

    def rot(z):
        z1, z2 = z[:, :half], z[:, half:]
        return jnp.concatenate([z1 * cos - z2 * sin, z1 * sin + z2 * cos], axis=1)

    for h in range(H_A):
        def proj(j, h=h):
            return _dot(hn, win_ref[:, j * d + h * dk:j * d + (h + 1) * dk])

        q = rot(proj(0)).astype(BF16)
        k = rot(proj(1)) * (dk ** -0.5)
        v = proj(2).astype(BF16)
        g = proj(3)
        for c in range(tc // chunk):
            r = slice(c * chunk, (c + 1) * chunk)
            qc, kc, vc = q[r], k[r], v[r]
            s_old = s_scr[h]
            scores = _dot_nt(qc, kc.astype(BF16)) * intra_ref[h]
            o = _dot(scores.astype(BF16), vc) + _dot(qc, s_old.astype(BF16)) * qdec_ref[h]
            s_scr[h] = s_old * cdec_ref[h] + _dot_tn((kc * kdec_ref[h]).astype(BF16), vc)
            mu = jnp.mean(o, axis=-1, keepdims=True)
            oc = o - mu
            var = jnp.mean(oc * oc, axis=-1, keepdims=True)
            on = oc * lax.rsqrt(var + GN_EPS) * gn_ref[:, h * dk:(h + 1) * dk]
            gc = g[r]
            gate_scr[r, h * dk:(h + 1) * dk] = (gc * _sigmoid(gc) * on).astype(BF16)

    h_ref[0] = x + _dot(gate_scr[...], wout_ref[...])

    @pl.when(t == pl.num_programs(1) - 1)
    def _():
        for h in range(H_A):
            for parity in range(2):
                for c in range(nblk):
                    sfin_ref[0, h, flat_rows(parity, c), :] = (
                        s_scr[h, parity * half:(parity + 1) * half, c * LANES:(c + 1) * LANES])


def _retention_layer(x, s0, cos, sin, g, w_in, gn, w_out, dec, *, tc, chunk):
    b, t, d = x.shape
    dk = d // H_A
    intra, qdec, kdec, cdec = dec
    nt = t // tc
    flat = (b, H_A, dk * dk // LANES, LANES)
    state_spec = pl.BlockSpec((1,) + flat[1:], lambda i, j: (i, 0, 0, 0))
    h_out, s_fin = pl.pallas_call(
        functools.partial(_ret_kernel, chunk=chunk),
        name="retention_layer",
        grid=(b, nt),
        in_specs=[
            pl.BlockSpec((1, tc, d), lambda i, j: (i, j, 0)),
            state_spec,
            pl.BlockSpec((tc, dk // 2), lambda i, j: (j, 0)),
            pl.BlockSpec((tc, dk // 2), lambda i, j: (j, 0)),
            _const_spec(g.shape), _const_spec(w_in.shape), _const_spec(gn.shape),
            _const_spec(w_out.shape), _const_spec(intra.shape), _const_spec(qdec.shape),
            _const_spec(kdec.shape), _const_spec(cdec.shape),
        ],
        out_specs=[
            pl.BlockSpec((1, tc, d), lambda i, j: (i, j, 0)),
            state_spec,
        ],
        out_shape=[jax.ShapeDtypeStruct((b, t, d), F32),
                   jax.ShapeDtypeStruct(flat, F32)],
        scratch_shapes=[pltpu.VMEM((H_A, dk, dk), F32), pltpu.VMEM((tc, d), BF16)],
        compiler_params=_params("parallel", "arbitrary"),
    )(x, s0.reshape(flat), cos, sin, g, w_in, gn, w_out, intra, qdec, kdec, cdec)
    return h_out, s_fin.reshape(b, H_A, dk, dk)


def _decay_tables(rows, c_true, dk):
    lg = jnp.log(1.0 - jnp.exp2(-5.0 - jnp.arange(H_A, dtype=F32)))
    i = jnp.arange(rows, dtype=F32)
    diff = i[:, None] - i[None, :]
    intra = jnp.where(diff >= 0, jnp.exp(lg[:, None, None] * jnp.maximum(diff, 0.0)), 0.0)
    qdec = jnp.exp(lg[:, None] * (i[None, :] + 1.0))
    kdec = jnp.exp(lg[:, None] * (c_true - 1.0 - i[None, :]))
    cdec = jnp.exp(lg * c_true)
    return (intra,
            jnp.broadcast_to(qdec[:, :, None], (H_A, rows, dk)),
            jnp.broadcast_to(kdec[:, :, None], (H_A, rows, dk)),
            jnp.broadcast_to(cdec[:, None, None], (H_A, 1, dk)))


def _rope_tables(pos, half):
    inv = 1.0 / (ROPE_BASE ** jnp.linspace(0.0, 1.0, half, dtype=F32))
    ang = pos.astype(F32)[:, None] * inv[None, :]
    return jnp.cos(ang), jnp.sin(ang)


def _ffn_kernel(*refs, ff, has_attn, final_norm):
    refs = list(refs)
    x_ref = refs.pop(0)
    if has_attn:
        a_ref, wo_ref = refs.pop(0), refs.pop(0)
    g_ref, wup_ref, wdown_ref = refs.pop(0), refs.pop(0), refs.pop(0)
    if final_norm:
        gfin_ref = refs.pop(0)
    o_ref, = refs

    x = x_ref[...]
    if has_attn:
        x = x + _dot(a_ref[...].astype(BF16), wo_ref[...])
    hn = _rmsnorm(x, g_ref[...]).astype(BF16)
    gate = _dot(hn, wup_ref[:, :ff])
    up = _dot(hn, wup_ref[:, ff:])
    act = (gate * _sigmoid(gate) * up).astype(BF16)
    y = x + _dot(act, wdown_ref[...])
    if final_norm:
        y = _rmsnorm(y, gfin_ref[...])
    o_ref[...] = y


def _ffn_block(x, g, w_up, w_down, *, tm, attn=None, g_final=None):
    m, d = x.shape
    ff = w_down.shape[0]
    row = lambda i: (i, 0)
    args, specs = [x], [pl.BlockSpec((tm, d), row)]
    if attn is not None:
        a, w_o = attn
        args += [a, w_o]
        specs += [pl.BlockSpec((tm, d), row), _const_spec(w_o.shape)]
    args += [g, w_up, w_down]
    specs += [_const_spec(g.shape), _const_spec(w_up.shape), _const_spec(w_down.shape)]
    if g_final is not None:
        args.append(g_final)
        specs.append(_const_spec(g_final.shape))
    return pl.pallas_call(
        functools.partial(_ffn_kernel, ff=ff, has_attn=attn is not None,
                          final_norm=g_final is not None),
        name="ffn_block",
        grid=(m // tm,),
        in_specs=specs,
        out_specs=pl.BlockSpec((tm, d), row),
        out_shape=jax.ShapeDtypeStruct((m, d), F32),
        compiler_params=_params("parallel"),
    )(*args)


def _proj_kernel(x_ref, gkv_ref, gq_ref, wkv_ref, wft_ref, bf_ref, tri_ref, wqg_ref,
                 k_ref, v_ref, kb_ref, vb_ref, lf_ref, c_ref, q_ref, gate_ref, carry_scr):
    t = pl.program_id(1)
    d = x_ref.shape[2]
    tm = x_ref.shape[1]

    @pl.when(t == 0)
    def _():
        carry_scr[...] = jnp.zeros_like(carry_scr)

    x = x_ref[0]
    xn = x * lax.rsqrt(jnp.mean(x * x, axis=-1, keepdims=True) + RMS_EPS)
    hkv = (xn * gkv_ref[...]).astype(BF16)
    hq = (xn * gq_ref[...]).astype(BF16)

    kv = _dot(hkv, wkv_ref[...])
    k, v = kv[:, :d], kv[:, d:]
    k_ref[0] = k
    v_ref[0] = v
    kb_ref[0] = k.astype(BF16)
    vb_ref[0] = v.astype(BF16)

    qg = _dot(hq, wqg_ref[...])
    q_ref[0] = qg[:, :d].astype(BF16)
    gate_ref[0] = qg[:, d:]

    lf = _log_sigmoid(_dot_nt(wft_ref[...], hkv) + bf_ref[...])
    lf_ref[0] = lf[:H_B]
    parts = _dot(jnp.concatenate(_split3(lf), axis=0), tri_ref[...])
    n = lf.shape[0]
    c = parts[:n] + parts[n:2 * n] + parts[2 * n:] + carry_scr[:, 0:1]
    c_ref[0] = c[:H_B]
    carry_scr[...] = jnp.broadcast_to(c[:, tm - 1:tm], carry_scr.shape)


def _projections(x, g_kv, g_q, w_kv, w_ft, b_f, w_qg, *, tm):
    b, t, d = x.shape
    tri = (jnp.arange(tm)[:, None] <= jnp.arange(tm)[None, :]).astype(BF16)
    tok = lambda i, j: (i, j, 0)
    hm = lambda i, j: (i, 0, j)
    tok_spec = pl.BlockSpec((1, tm, d), tok)
    hm_spec = pl.BlockSpec((1, H_B, tm), hm)
    return pl.pallas_call(
        _proj_kernel,
        name="kv_qg_projections",
        grid=(b, t // tm),
        in_specs=[tok_spec, _const_spec(g_kv.shape), _const_spec(g_q.shape), _const_spec(w_kv.shape),
                  _const_spec(w_ft.shape), _const_spec(b_f.shape), _const_spec(tri.shape),
                  _const_spec(w_qg.shape)],
        out_specs=[tok_spec, tok_spec, tok_spec, tok_spec, hm_spec, hm_spec, tok_spec, tok_spec],
        out_shape=[jax.ShapeDtypeStruct((b, t, d), F32), jax.ShapeDtypeStruct((b, t, d), F32),
                   jax.ShapeDtypeStruct((b, t, d), BF16), jax.ShapeDtypeStruct((b, t, d), BF16),
                   jax.ShapeDtypeStruct((b, H_B, t), F32), jax.ShapeDtypeStruct((b, H_B, t), F32),
                   jax.ShapeDtypeStruct((b, t, d), BF16), jax.ShapeDtypeStruct((b, t, d), F32)],
        scratch_shapes=[pltpu.VMEM((ROWS_BF16, LANES), F32)],
        compiler_params=_params("parallel", "arbitrary"),
    )(x, g_kv, g_q, w_kv, w_ft, b_f, tri, w_qg)


def _attn_kernel(q_ref, k_ref, v_ref, c_ref, gate_ref, o_ref, *, scale):
    qi = pl.program_id(2)
    tq, dh = q_ref.shape[1], q_ref.shape[2]
    q = q_ref[0]

    def block(kj, carry, diagonal):
        m, l, acc = carry
        start = pl.multiple_of(kj * tq, tq)
        ks = k_ref[0, pl.ds(start, tq), :]
        vs = v_ref[0, pl.ds(start, tq), :]
        s = _dot_nt(q, ks) * scale - c_ref[0, 0, pl.ds(kj, 1), :]
        if diagonal:
            row = lax.broadcasted_iota(jnp.int32, (tq, tq), 0)
            col = lax.broadcasted_iota(jnp.int32, (tq, tq), 1)
            s = jnp.where(col <= row, s, -jnp.inf)
        m_new = jnp.maximum(m, jnp.max(s, axis=-1, keepdims=True))
        p = jnp.exp(s - m_new)
        alpha = jnp.exp(m - m_new)
        l = alpha * l + jnp.sum(p, axis=-1, keepdims=True)
        acc = alpha * acc + _dot(p.astype(BF16), vs)
        return m_new, l, acc

    init = (jnp.full((tq, 1), -jnp.inf, F32), jnp.zeros((tq, 1), F32), jnp.zeros((tq, dh), F32))
    carry = lax.fori_loop(0, qi, lambda kj, cr: block(kj, cr, False), init)
    _, l, acc = block(qi, carry, True)
    o_ref[0] = (_sigmoid(gate_ref[0]) * (acc / l)).astype(o_ref.dtype)


def _prompt_attention(q, kb, vb, c, gate, *, tq):
    b, s, d = q.shape
    dh = d // H_B
    nq = s // tq
    c4 = c.reshape(b, H_B, nq, tq)
    qspec = pl.BlockSpec((1, tq, dh), lambda i, h, j: (i, j, h))
    kvspec = pl.BlockSpec((1, s, dh), lambda i, h, j: (i, 0, h))
    return pl.pallas_call(
        functools.partial(_attn_kernel, scale=dh ** -0.5),
        name="prompt_attention",
        grid=(b, H_B, nq),
        in_specs=[qspec, kvspec, kvspec,
                  pl.BlockSpec((1, 1, nq, tq), lambda i, h, j: (i, h, 0, 0)), qspec],
        out_specs=qspec,
        out_shape=jax.ShapeDtypeStruct((b, s, d), BF16),
        compiler_params=_params("parallel", "parallel", "arbitrary"),
    )(q, kb, vb, c4, gate)


def _lane_cumsum(x):
    lane = lax.broadcasted_iota(jnp.int32, x.shape, 1)
    shift = 1
    while shift < x.shape[1]:
        x = x + jnp.where(lane >= shift, pltpu.roll(x, shift, axis=1), 0.0)
        shift *= 2
    return x


def _sample_attn_kernel(pt_ref, q_ref, gate_ref, knew_ref, vnew_ref, lfnew_ref, *refs,
                        group, t_new, scale):
    k_refs = refs[:group]
    v_refs = refs[group:2 * group]
    lf_refs = refs[2 * group:3 * group]
    o_ref = refs[3 * group]
    qb_scr, m_scr, l_scr, acc_scr, carry_scr = refs[3 * group + 1:]
    b = pl.program_id(0)
    p = pl.program_id(1)
    d = q_ref.shape[2]
    dh = d // H_B
    nrow = t_new * H_B
    head_mask = (lax.broadcasted_iota(jnp.int32, (H_B, d), 1) // dh
                 == lax.broadcasted_iota(jnp.int32, (H_B, d), 0))

    @pl.when(p == 0)
    def _():
        qv = q_ref[0]
        for t in range(t_new):
            qb_scr[t * H_B:(t + 1) * H_B, :] = jnp.where(head_mask, qv[t:t + 1, :], 0.0)
        m_scr[...] = jnp.full_like(m_scr, -jnp.inf)
        l_scr[...] = jnp.zeros_like(l_scr)
        acc_scr[...] = jnp.zeros_like(acc_scr)
        carry_scr[...] = jnp.zeros_like(carry_scr)

    qb = qb_scr[...].astype(BF16)

    def update(s, vals):
        m_old = m_scr[:, 0:1]
        m_new = jnp.maximum(m_old, jnp.max(s, axis=-1, keepdims=True))
        pr = jnp.exp(s - m_new)
        alpha = jnp.exp(m_old - m_new)
        l_scr[...] = jnp.broadcast_to(alpha * l_scr[:, 0:1] + jnp.sum(pr, axis=-1, keepdims=True),
                                      l_scr.shape)
        acc_scr[...] = alpha * acc_scr[...] + _dot(pr.astype(BF16), vals)
        m_scr[...] = jnp.broadcast_to(m_new, m_scr.shape)

    carry = carry_scr[...]
    biases = []
    for gi in range(group):
        w = _lane_cumsum(lf_refs[gi][...])
        biases.append(jnp.tile(carry + w, (t_new, 1)))
        carry = carry + w[:, w.shape[1] - 1:]
    carry_scr[...] = carry
    keys = jnp.concatenate([r[...].astype(BF16) for r in k_refs], axis=0)
    vals = jnp.concatenate([r[...].astype(BF16) for r in v_refs], axis=0)
    s = _dot_nt(qb, keys) * scale - jnp.concatenate(biases, axis=1)
    update(s, vals)

    @pl.when(p == pl.num_programs(1) - 1)
    def _():
        n = knew_ref.shape[0]
        col = lax.broadcasted_iota(jnp.int32, (H_B, n), 1)
        x = lfnew_ref[...]
        shift = 1
        while shift < t_new:
            x = x + jnp.where(col % t_new >= shift, pltpu.roll(x, shift, axis=1), 0.0)
            shift *= 2
        bias = jnp.tile(carry + x, (t_new, 1))
        colr = lax.broadcasted_iota(jnp.int32, (nrow, n), 1)
        rowr = lax.broadcasted_iota(jnp.int32, (nrow, n), 0)
        visible = (colr // t_new == b) & (colr % t_new <= rowr // H_B)
        s_new = _dot_nt(qb, knew_ref[...]) * scale - bias
        update(jnp.where(visible, s_new, -jnp.inf), vnew_ref[...])
        o = acc_scr[...] / l_scr[:, 0:1]
        rows = [jnp.sum(jnp.where(head_mask, o[t * H_B:(t + 1) * H_B, :], 0.0), axis=0, keepdims=True)
                for t in range(t_new)]
        o_ref[0] = _sigmoid(gate_ref[0]) * jnp.concatenate(rows, axis=0)


def _sample_attention(page_table, q, gate, k_new, v_new, lf_new, cache_k, cache_v, cache_lf_t, *, group):
    bd, t_new, d = q.shape
    n_pages = page_table.shape[1]
    page = cache_k.shape[1]
    nrow = t_new * H_B
    per_b = pl.BlockSpec((1, t_new, d), lambda i, j, pt: (i, 0, 0))
    whole = lambda a: pl.BlockSpec(a.shape, lambda i, j, pt: (0,) * a.ndim)

    def paged(shape, gi):
        return pl.BlockSpec((None,) + shape, lambda i, j, pt: (pt[i, j * group + gi], 0, 0))

    in_specs = [per_b, per_b, whole(k_new), whole(v_new), whole(lf_new)]
    in_specs += [paged((page, d), gi) for gi in range(group)]
    in_specs += [paged((page, d), gi) for gi in range(group)]
    in_specs += [paged((H_B, page), gi) for gi in range(group)]
    grid_spec = pltpu.PrefetchScalarGridSpec(
        num_scalar_prefetch=1,
        grid=(bd, n_pages // group),
        in_specs=in_specs,
        out_specs=per_b,
        scratch_shapes=[pltpu.VMEM((nrow, d), F32), pltpu.VMEM((nrow, LANES), F32),
                        pltpu.VMEM((nrow, LANES), F32), pltpu.VMEM((nrow, d), F32),
                        pltpu.VMEM((H_B, LANES), F32)],
    )
    return pl.pallas_call(
        functools.partial(_sample_attn_kernel, group=group, t_new=t_new, scale=(d // H_B) ** -0.5),
        name="sample_attention",
        grid_spec=grid_spec,
        out_shape=jax.ShapeDtypeStruct((bd, t_new, d), F32),
        compiler_params=_params("parallel", "arbitrary"),
    )(page_table, q, gate, k_new, v_new, lf_new,
      *([cache_k] * group), *([cache_v] * group), *([cache_lf_t] * group))


def _pick_tile(n, target):
    t = min(n, target)
    while n % t:
        t //= 2
    return t


def kernel(x_prompt, x_sample, state_ret, cache_k, cache_v, cache_logf, page_table,
           norm_mix, w_ret_in, gn_ret, w_ret_out, norm_kv, w_kvf, b_f,
           w_b_qg, w_b_out, norm_ffn, w_ffn_up, w_ffn_down, norm_final):
    b, t, d = x_prompt.shape
    bd, td, _ = x_sample.shape
    dk = d // H_A
    dh = d // H_B
    n_pool, page = cache_k.shape[0], cache_k.shape[1]
    past = page_table.shape[1] * page

    perm = jnp.concatenate([jnp.arange(0, dk, 2), jnp.arange(1, dk, 2)])
    head_cols = (jnp.arange(H_A)[:, None] * dk + perm[None, :]).reshape(-1)
    cols = jnp.concatenate([head_cols, d + head_cols, 2 * d + jnp.arange(2 * d)])
    w_in = w_ret_in[0][:, cols].astype(BF16)
    w_out = w_ret_out[0].astype(BF16)
    w_up = [w_ffn_up[i].astype(BF16) for i in range(2)]
    w_down = [w_ffn_down[i].astype(BF16) for i in range(2)]
    w_kv = w_kvf[:, :2 * d].astype(BF16)
    w_ft = jnp.zeros((ROWS_BF16, d), BF16).at[:H_B].set(w_kvf[:, 2 * d:].T.astype(BF16))
    b_col = jnp.zeros((ROWS_BF16, 1), F32).at[:H_B, 0].set(b_f.astype(F32))
    w_qg = w_b_qg[0].astype(BF16)
    w_o = w_b_out[0].astype(BF16)
    row = lambda v: v.reshape(1, d).astype(F32)
    g_mix0, g_mix1 = row(norm_mix[0]), row(norm_mix[1])
    g_ffn0, g_ffn1 = row(norm_ffn[0]), row(norm_ffn[1])
    g_kv, g_fin, gn = row(norm_kv), row(norm_final), row(gn_ret[0])

    def trunk_tokens(h1, m_tile):
        bb, tt, _ = h1.shape
        h2 = _ffn_block(h1.reshape(bb * tt, d), g_ffn0, w_up[0], w_down[0], tm=m_tile).reshape(bb, tt, d)
        return h2, _projections(h2, g_kv, g_mix1, w_kv, w_ft, b_col, w_qg, tm=m_tile)

    tc = _pick_tile(t, 512)
    chunk = _pick_tile(tc, 256)
    cos_p, sin_p = _rope_tables(jnp.arange(t), dk // 2)
    h1, s_prompt = _retention_layer(
        x_prompt, jnp.zeros((b, H_A, dk, dk), F32), cos_p, sin_p, g_mix0, w_in, gn, w_out,
        _decay_tables(chunk, chunk, dk), tc=tc, chunk=chunk)
    h2, (k_p, v_p, kb_p, vb_p, lf_p, c_p, q_p, gate_p) = trunk_tokens(h1, tc)
    a_p = _prompt_attention(q_p, kb_p, vb_p, c_p, gate_p, tq=_pick_tile(t, 256))
    y_prompt = _ffn_block(h2.reshape(b * t, d), g_ffn1, w_up[1], w_down[1], tm=tc,
                          attn=(a_p.reshape(b * t, d), w_o), g_final=g_fin).reshape(b, t, d)

    rows = LANES
    xs = jnp.zeros((bd, rows, d), x_sample.dtype).at[:, :td].set(x_sample)
    cos_s, sin_s = _rope_tables(past + jnp.arange(rows), dk // 2)
    h1s, s_sample = _retention_layer(
        xs, state_ret[0], cos_s, sin_s, g_mix0, w_in, gn, w_out,
        _decay_tables(rows, td, dk), tc=rows, chunk=rows)
    h1s = h1s[:, :td].reshape(1, bd * td, d)
    h2s, (k_s, v_s, kb_s, vb_s, lf_s, _, q_s, gate_s) = trunk_tokens(h1s, bd * td)
    cache_lf_t = jnp.swapaxes(cache_logf, 1, 2)
    a_s = _sample_attention(
        page_table, q_s.astype(F32).reshape(bd, td, d), gate_s.reshape(bd, td, d),
        kb_s[0], vb_s[0], lf_s[0],
        cache_k.reshape(n_pool, page, d), cache_v.reshape(n_pool, page, d), cache_lf_t,
        group=8)
    y_sample = _ffn_block(h2s.reshape(bd * td, d), g_ffn1, w_up[1], w_down[1], tm=bd * td,
                          attn=(a_s.reshape(bd * td, d), w_o), g_final=g_fin).reshape(bd, td, d)

    return (y_prompt, y_sample, s_prompt[None],
            k_p.reshape(b, t, H_B, dh), v_p.reshape(b, t, H_B, dh), jnp.swapaxes(lf_p, 1, 2),
            s_sample[None],
            k_s.reshape(bd, td, H_B, dh), v_s.reshape(bd, td, H_B, dh),
            jnp.swapaxes(lf_s[0], 0, 1).reshape(bd, td, H_B))
```

```python
import functools
import math

import jax
import jax.numpy as jnp
from jax import lax
from jax.experimental import pallas as pl
from jax.experimental.pallas import tpu as pltpu

H_A = 4
H_B = 8
RMS_EPS = 1e-6
GN_EPS = 1e-5
ROPE_BASE = 10000.0
LOG2E = math.log2(math.e)
LANES = 128
ROWS_BF16 = 16
VMEM_LIMIT_BYTES = 56 * 1024 * 1024

F32 = jnp.float32
BF16 = jnp.bfloat16


def _dot(a, b):
    return jnp.dot(a, b, preferred_element_type=F32)


def _dot_nt(a, b):
    return lax.dot_general(a, b, (((1,), (1,)), ((), ())), preferred_element_type=F32)


def _dot_tn(a, b):
    return lax.dot_general(a, b, (((0,), (0,)), ((), ())), preferred_element_type=F32)


def _rmsnorm(x, g):
    return x * lax.rsqrt(jnp.mean(x * x, axis=-1, keepdims=True) + RMS_EPS) * g


def _sigmoid(x):
    return 1.0 / (1.0 + jnp.exp(-x))


def _log_sigmoid(x):
    return jnp.minimum(x, 0.0) - jnp.log1p(jnp.exp(-jnp.abs(x)))


def _split3(x):
    hi = x.astype(BF16)
    r1 = x - hi.astype(F32)
    mid = r1.astype(BF16)
    lo = (r1 - mid.astype(F32)).astype(BF16)
    return hi, mid, lo


def _const_spec(shape):
    nd = len(shape)
    return pl.BlockSpec(shape, lambda *_: (0,) * nd, pipeline_mode=pl.Buffered(1))


def _params(*sem):
    return pltpu.CompilerParams(dimension_semantics=sem, vmem_limit_bytes=VMEM_LIMIT_BYTES)


def _permute_rows(p_ref, s):
    p = p_ref[...]
    hi, mid, lo = _split3(s)
    return _dot(p, hi) + _dot(p, mid) + _dot(p, lo)


def _ret_kernel(*refs, chunk, has_state):
    refs = list(refs)
    x_ref = refs.pop(0)
    s0_ref = refs.pop(0) if has_state else None
    (cos_ref, sin_ref, g_ref, win_ref, gn_ref, wout_ref, intra_ref, qdec_ref, kdec_ref, cdec_ref,
     pin_ref, pout_ref, h_ref, sfin_ref, s_scr, gate_scr) = refs
    t = pl.program_id(1)
    tc, d = x_ref.shape[1], x_ref.shape[2]
    dk = d // H_A
    half = dk // 2

    @pl.when(t == 0)
    def _():
        for h in range(H_A):
            if has_state:
                s_scr[h] = _permute_rows(pin_ref, s0_ref[0, h])
            else:
                s_scr[h] = jnp.zeros((dk, dk), F32)

    x = x_ref[0]
    hn = _rmsnorm(x, g_ref[...]).astype(BF16)
    cos = cos_ref[...]
    sin = sin_ref[...]

    def rot(z):
        z1, z2 = z[:, :half], z[:, half:]
        return jnp.concatenate([z1 * cos - z2 * sin, z1 * sin + z2 * cos], axis=1)

    for h in range(H_A):
        def proj(j, h=h):
            return _dot(hn, win_ref[:, j * d + h * dk:j * d + (h + 1) * dk])

        q = rot(proj(0)).astype(BF16)
        k = rot(proj(1)) * (dk ** -0.5)
        v = proj(2).astype(BF16)
        g = proj(3)
        for c in range(tc // chunk):
            r = slice(c * chunk, (c + 1) * chunk)
            qc, kc, vc = q[r], k[r], v[r]
            s_old = s_scr[h]
            scores = _dot_nt(qc, kc.astype(BF16)) * intra_ref[h]
            o = _dot(scores.astype(BF16), vc) + _dot(qc, s_old.astype(BF16)) * qdec_ref[h]
            s_scr[h] = s_old * cdec_ref[h] + _dot_tn((kc * kdec_ref[h]).astype(BF16), vc)
            mu = jnp.mean(o, axis=-1, keepdims=True)
            oc = o - mu
            var = jnp.mean(oc * oc, axis=-1, keepdims=True)
            on = oc * lax.rsqrt(var + GN_EPS) * gn_ref[:, h * dk:(h + 1) * dk]
            gc = g[r]
            gate_scr[r, h * dk:(h + 1) * dk] = (gc * _sigmoid(gc) * on).astype(BF16)

    h_ref[0] = x + _dot(gate_scr[...], wout_ref[...])

    @pl.when(t == pl.num_programs(1) - 1)
    def _():
        for h in range(H_A):
            sfin_ref[0, h] = _permute_rows(pout_ref, s_scr[h])


def _retention_layer(x, s0, cos, sin, g, w_in, gn, w_out, dec, perms, *, tc, chunk):
    b, t, d = x.shape
    dk = d // H_A
    state_spec = pl.BlockSpec((1, H_A, dk, dk), lambda i, j: (i, 0, 0, 0))
    tok_spec = pl.BlockSpec((1, tc, d), lambda i, j: (i, j, 0))
    rope_spec = pl.BlockSpec((tc, dk // 2), lambda i, j: (j, 0))
    consts = [g, w_in, gn, w_out, *dec, *perms]
    args = [x] + ([] if s0 is None else [s0]) + [cos, sin] + consts
    specs = ([tok_spec] + ([] if s0 is None else [state_spec]) + [rope_spec, rope_spec]
             + [_const_spec(a.shape) for a in consts])
    return pl.pallas_call(
        functools.partial(_ret_kernel, chunk=chunk, has_state=s0 is not None),
        name="retention_layer",
        grid=(b, t // tc),
        in_specs=specs,
        out_specs=[tok_spec, state_spec],
        out_shape=[jax.ShapeDtypeStruct((b, t, d), F32),
                   jax.ShapeDtypeStruct((b, H_A, dk, dk), F32)],
        scratch_shapes=[pltpu.VMEM((H_A, dk, dk), F32), pltpu.VMEM((tc, d), BF16)],
        compiler_params=_params("parallel", "arbitrary"),
    )(*args)


def _decay_tables(rows, c_true, dk):
    lg = jnp.log(1.0 - jnp.exp2(-5.0 - jnp.arange(H_A, dtype=F32)))
    i = jnp.arange(rows, dtype=F32)
    diff = i[:, None] - i[None, :]
    intra = jnp.where(diff >= 0, jnp.exp(lg[:, None, None] * jnp.maximum(diff, 0.0)), 0.0)
    qdec = jnp.exp(lg[:, None] * (i[None, :] + 1.0))
    kdec = jnp.exp(lg[:, None] * (c_true - 1.0 - i[None, :]))
    cdec = jnp.exp(lg * c_true)
    return (intra,
            jnp.broadcast_to(qdec[:, :, None], (H_A, rows, dk)),
            jnp.broadcast_to(kdec[:, :, None], (H_A, rows, dk)),
            jnp.broadcast_to(cdec[:, None, None], (H_A, 1, dk)))


def _rope_tables(pos, half):
    inv = 1.0 / (ROPE_BASE ** jnp.linspace(0.0, 1.0, half, dtype=F32))
    ang = pos.astype(F32)[:, None] * inv[None, :]
    return jnp.cos(ang), jnp.sin(ang)


def _ffn_kernel(*refs, ff, has_attn, final_norm):
    refs = list(refs)
    x_ref = refs.pop(0)
    if has_attn:
        a_ref, wo_ref = refs.pop(0), refs.pop(0)
    g_ref, wup_ref, wdown_ref = refs.pop(0), refs.pop(0), refs.pop(0)
    if final_norm:
        gfin_ref = refs.pop(0)
    o_ref, = refs

    x = x_ref[...]
    if has_attn:
        x = x + _dot(a_ref[...].astype(BF16), wo_ref[...])
    hn = _rmsnorm(x, g_ref[...]).astype(BF16)
    gate = _dot(hn, wup_ref[:, :ff])
    up = _dot(hn, wup_ref[:, ff:])
    act = (gate * _sigmoid(gate) * up).astype(BF16)
    y = x + _dot(act, wdown_ref[...])
    if final_norm:
        y = _rmsnorm(y, gfin_ref[...])
    o_ref[...] = y


def _ffn_block(x, g, w_up, w_down, *, tm, attn=None, g_final=None):
    m, d = x.shape
    ff = w_down.shape[0]
    row = lambda i: (i, 0)
    args, specs = [x], [pl.BlockSpec((tm, d), row)]
    if attn is not None:
        a, w_o = attn
        args += [a, w_o]
        specs += [pl.BlockSpec((tm, d), row), _const_spec(w_o.shape)]
    args += [g, w_up, w_down]
    specs += [_const_spec(g.shape), _const_spec(w_up.shape), _const_spec(w_down.shape)]
    if g_final is not None:
        args.append(g_final)
        specs.append(_const_spec(g_final.shape))
    return pl.pallas_call(
        functools.partial(_ffn_kernel, ff=ff, has_attn=attn is not None,
                          final_norm=g_final is not None),
        name="ffn_block",
        grid=(m // tm,),
        in_specs=specs,
        out_specs=pl.BlockSpec((tm, d), row),
        out_shape=jax.ShapeDtypeStruct((m, d), F32),
        compiler_params=_params("parallel"),
    )(*args)


def _proj_kernel(x_ref, gkv_ref, gq_ref, wkv_ref, wft_ref, bf_ref, tri_ref, wqg_ref,
                 k_ref, v_ref, kb_ref, vb_ref, lf_ref, c_ref, q_ref, gate_ref, carry_scr,
                 *, q_scale, c_scale):
    t = pl.program_id(1)
    d = x_ref.shape[2]
    tm = x_ref.shape[1]

    @pl.when(t == 0)
    def _():
        carry_scr[...] = jnp.zeros_like(carry_scr)

    x = x_ref[0]
    xn = x * lax.rsqrt(jnp.mean(x * x, axis=-1, keepdims=True) + RMS_EPS)
    hkv = (xn * gkv_ref[...]).astype(BF16)
    hq = (xn * gq_ref[...]).astype(BF16)

    kv = _dot(hkv, wkv_ref[...])
    k, v = kv[:, :d], kv[:, d:]
    k_ref[0] = k
    v_ref[0] = v
    kb_ref[0] = k.astype(BF16)
    vb_ref[0] = v.astype(BF16)

    qg = _dot(hq, wqg_ref[...])
    q_ref[0] = (qg[:, :d] * q_scale).astype(BF16)
    gate_ref[0] = qg[:, d:]

    lf = _log_sigmoid(_dot_nt(wft_ref[...], hkv) + bf_ref[...])
    lf_ref[0] = lf[:H_B]
    parts = _dot(jnp.concatenate(_split3(lf), axis=0), tri_ref[...])
    n = lf.shape[0]
    c = parts[:n] + parts[n:2 * n] + parts[2 * n:] + carry_scr[:, 0:1]
    c_ref[0] = c[:H_B] * c_scale
    carry_scr[...] = jnp.broadcast_to(c[:, tm - 1:tm], carry_scr.shape)


def _projections(x, g_kv, g_q, w_kv, w_ft, b_f, w_qg, *, tm, q_scale, c_scale):
    b, t, d = x.shape
    tri = (jnp.arange(tm)[:, None] <= jnp.arange(tm)[None, :]).astype(BF16)
    tok = lambda i, j: (i, j, 0)
    hm = lambda i, j: (i, 0, j)
    tok_spec = pl.BlockSpec((1, tm, d), tok)
    hm_spec = pl.BlockSpec((1, H_B, tm), hm)
    return pl.pallas_call(
        functools.partial(_proj_kernel, q_scale=q_scale, c_scale=c_scale),
        name="kv_qg_projections",
        grid=(b, t // tm),
        in_specs=[tok_spec, _const_spec(g_kv.shape), _const_spec(g_q.shape), _const_spec(w_kv.shape),
                  _const_spec(w_ft.shape), _const_spec(b_f.shape), _const_spec(tri.shape),
                  _const_spec(w_qg.shape)],
        out_specs=[tok_spec, tok_spec, tok_spec, tok_spec, hm_spec, hm_spec, tok_spec, tok_spec],
        out_shape=[jax.ShapeDtypeStruct((b, t, d), F32), jax.ShapeDtypeStruct((b, t, d), F32),
                   jax.ShapeDtypeStruct((b, t, d), BF16), jax.ShapeDtypeStruct((b, t, d), BF16),
                   jax.ShapeDtypeStruct((b, H_B, t), F32), jax.ShapeDtypeStruct((b, H_B, t), F32),
                   jax.ShapeDtypeStruct((b, t, d), BF16), jax.ShapeDtypeStruct((b, t, d), F32)],
        scratch_shapes=[pltpu.VMEM((ROWS_BF16, LANES), F32)],
        compiler_params=_params("parallel", "arbitrary"),
    )(x, g_kv, g_q, w_kv, w_ft, b_f, tri, w_qg)


def _attn_kernel(q_ref, k_ref, v_ref, c_ref, gate_ref, o_ref):
    qi = pl.program_id(2)
    tq, dh = q_ref.shape[1], q_ref.shape[2]
    q = q_ref[0]

    def block(kj, carry, diagonal):
        m, l, acc = carry
        start = pl.multiple_of(kj * tq, tq)
        ks = k_ref[0, pl.ds(start, tq), :]
        vs = v_ref[0, pl.ds(start, tq), :]
        s = _dot_nt(q, ks) - c_ref[0, 0, pl.ds(kj, 1), :]
        if diagonal:
            row = lax.broadcasted_iota(jnp.int32, (tq, tq), 0)
            col = lax.broadcasted_iota(jnp.int32, (tq, tq), 1)
            s = jnp.where(col <= row, s, -jnp.inf)
        m_new = jnp.maximum(m, jnp.max(s, axis=-1, keepdims=True))
        p = jnp.exp2(s - m_new)
        alpha = jnp.exp2(m - m_new)
        l = alpha * l + jnp.sum(p, axis=-1, keepdims=True)
        acc = alpha * acc + _dot(p.astype(BF16), vs)
        return m_new, l, acc

    init = (jnp.full((tq, 1), -jnp.inf, F32), jnp.zeros((tq, 1), F32), jnp.zeros((tq, dh), F32))
    carry = lax.fori_loop(0, qi, lambda kj, cr: block(kj, cr, False), init)
    _, l, acc = block(qi, carry, True)
    o_ref[0] = (_sigmoid(gate_ref[0]) * (acc / l)).astype(o_ref.dtype)


def _prompt_attention(q, kb, vb, c, gate, *, tq):
    b, s, d = q.shape
    dh = d // H_B
    nq = s // tq
    c4 = c.reshape(b, H_B, nq, tq)
    qspec = pl.BlockSpec((1, tq, dh), lambda i, h, j: (i, j, h))
    kvspec = pl.BlockSpec((1, s, dh), lambda i, h, j: (i, 0, h))
    return pl.pallas_call(
        _attn_kernel,
        name="prompt_attention",
        grid=(b, H_B, nq),
        in_specs=[qspec, kvspec, kvspec,
                  pl.BlockSpec((1, 1, nq, tq), lambda i, h, j: (i, h, 0, 0)), qspec],
        out_specs=qspec,
        out_shape=jax.ShapeDtypeStruct((b, s, d), BF16),
        compiler_params=_params("parallel", "parallel", "arbitrary"),
    )(q, kb, vb, c4, gate)


def _sample_attn_kernel(pt_ref, q_ref, gate_ref, knew_ref, vnew_ref, lfnew_ref, *refs,
                        group, t_new):
    k_refs = refs[:group]
    v_refs = refs[group:2 * group]
    lf_refs = refs[2 * group:3 * group]
    o_ref = refs[3 * group]
    m_scr, l_scr, acc_scr, carry_scr = refs[3 * group + 1:]
    b = pl.program_id(0)
    p = pl.program_id(1)
    nrow = q_ref.shape[0]
    ncol = k_refs[0].shape[0]
    stride = H_B
    q = q_ref[...]

    @pl.when(p == 0)
    def _():
        m_scr[...] = jnp.full_like(m_scr, -jnp.inf)
        l_scr[...] = jnp.zeros_like(l_scr)
        acc_scr[...] = jnp.zeros_like(acc_scr)
        carry_scr[...] = jnp.zeros_like(carry_scr)

    def same_head(n):
        row = lax.broadcasted_iota(jnp.int32, (nrow, n), 0)
        col = lax.broadcasted_iota(jnp.int32, (nrow, n), 1)
        return row % stride == col % stride, row, col

    def update(scores, values):
        m_old = m_scr[:, 0:1]
        m_new = m_old
        for s in scores:
            m_new = jnp.maximum(m_new, jnp.max(s, axis=-1, keepdims=True))
        alpha = jnp.exp2(m_old - m_new)
        l_new = alpha * l_scr[:, 0:1]
        acc = alpha * acc_scr[...]
        for s, v in zip(scores, values):
            pr = jnp.exp2(s - m_new)
            l_new = l_new + jnp.sum(pr, axis=-1, keepdims=True)
            acc = acc + _dot(pr.astype(BF16), v)
        l_scr[...] = jnp.broadcast_to(l_new, l_scr.shape)
        acc_scr[...] = acc
        m_scr[...] = jnp.broadcast_to(m_new, m_scr.shape)

    lf = jnp.concatenate([r[...] for r in lf_refs], axis=0) * LOG2E
    lane = lax.broadcasted_iota(jnp.int32, lf.shape, 1)
    within = lf
    shift = stride
    while shift < ncol:
        within = within + jnp.where(lane >= shift, pltpu.roll(within, shift, axis=1), 0.0)
        shift *= 2
    total = jnp.where(lane >= ncol - stride, within, 0.0)
    shift = stride
    while shift < ncol:
        total = total + jnp.where(lane < ncol - shift, pltpu.roll(total, ncol - shift, axis=1), 0.0)
        shift *= 2

    mask, _, _ = same_head(ncol)
    carry = carry_scr[0:1, :]
    scores, values = [], []
    for gi in range(group):
        kb = k_refs[gi][...].astype(BF16)
        s = _dot_nt(q, kb) - (carry + within[gi:gi + 1, :])
        scores.append(jnp.where(mask, s, -jnp.inf))
        values.append(v_refs[gi][...].astype(BF16))
        carry = carry + total[gi:gi + 1, :]
    carry_scr[...] = jnp.broadcast_to(carry, carry_scr.shape)
    update(scores, values)

    @pl.when(p == pl.num_programs(1) - 1)
    def _():
        n = knew_ref.shape[0]
        x = lfnew_ref[...] * LOG2E
        col1 = lax.broadcasted_iota(jnp.int32, x.shape, 1)
        shift = 1
        while shift < t_new:
            x = x + jnp.where((col1 // stride) % t_new >= shift,
                              pltpu.roll(x, shift * stride, axis=1), 0.0)
            shift *= 2
        bias = carry[:, :n] + x
        mask_new, row, col = same_head(n)
        tok = col // stride
        visible = mask_new & (tok // t_new == b) & (tok % t_new <= row // stride)
        s_new = _dot_nt(q, knew_ref[...]) - bias
        update([jnp.where(visible, s_new, -jnp.inf)], [vnew_ref[...]])
        o_ref[...] = _sigmoid(gate_ref[...]) * (acc_scr[...] / l_scr[:, 0:1])


def _sample_attention(page_table, q, gate, k_new, v_new, lf_new, cache_k, cache_v, cache_lf, *, group, t_new):
    n_rows, dh = q.shape
    nrow = t_new * H_B
    bd = n_rows // nrow
    n_pages = page_table.shape[1]
    ncol = cache_k.shape[1]
    per_b = pl.BlockSpec((nrow, dh), lambda i, j, pt: (i, 0))
    whole = lambda a: pl.BlockSpec(a.shape, lambda i, j, pt: (0,) * a.ndim)

    def paged(shape, gi):
        return pl.BlockSpec((None,) + shape, lambda i, j, pt: (pt[i, j * group + gi], 0, 0))

    in_specs = [per_b, per_b, whole(k_new), whole(v_new), whole(lf_new)]
    in_specs += [paged((ncol, dh), gi) for gi in range(group)]
    in_specs += [paged((ncol, dh), gi) for gi in range(group)]
    in_specs += [paged((1, ncol), gi) for gi in range(group)]
    grid_spec = pltpu.PrefetchScalarGridSpec(
        num_scalar_prefetch=1,
        grid=(bd, n_pages // group),
        in_specs=in_specs,
        out_specs=per_b,
        scratch_shapes=[pltpu.VMEM((nrow, LANES), F32), pltpu.VMEM((nrow, LANES), F32),
                        pltpu.VMEM((nrow, dh), F32), pltpu.VMEM((8, ncol), F32)],
    )
    return pl.pallas_call(
        functools.partial(_sample_attn_kernel, group=group, t_new=t_new),
        name="sample_attention",
        grid_spec=grid_spec,
        out_shape=jax.ShapeDtypeStruct((n_rows, dh), F32),
        compiler_params=_params("parallel", "arbitrary"),
    )(page_table, q, gate, k_new, v_new, lf_new,
      *([cache_k] * group), *([cache_v] * group), *([cache_lf] * group))


def _pick_tile(n, target):
    t = min(n, target)
    while n % t:
        t //= 2
    return t


def kernel(x_prompt, x_sample, state_ret, cache_k, cache_v, cache_logf, page_table,
           norm_mix, w_ret_in, gn_ret, w_ret_out, norm_kv, w_kvf, b_f,
           w_b_qg, w_b_out, norm_ffn, w_ffn_up, w_ffn_down, norm_final):
    b, t, d = x_prompt.shape
    bd, td, _ = x_sample.shape
    dk = d // H_A
    dh = d // H_B
    n_pool, page = cache_k.shape[0], cache_k.shape[1]
    past = page_table.shape[1] * page

    perm = jnp.concatenate([jnp.arange(0, dk, 2), jnp.arange(1, dk, 2)])
    head_cols = (jnp.arange(H_A)[:, None] * dk + perm[None, :]).reshape(-1)
    cols = jnp.concatenate([head_cols, d + head_cols, 2 * d + jnp.arange(2 * d)])
    w_in = w_ret_in[0][:, cols].astype(BF16)
    perm_in = (perm[:, None] == jnp.arange(dk)[None, :]).astype(BF16)
    perms = (perm_in, perm_in.T)
    w_out = w_ret_out[0].astype(BF16)
    w_up = [w_ffn_up[i].astype(BF16) for i in range(2)]
    w_down = [w_ffn_down[i].astype(BF16) for i in range(2)]
    w_kv = w_kvf[:, :2 * d].astype(BF16)
    w_ft = jnp.zeros((ROWS_BF16, d), BF16).at[:H_B].set(w_kvf[:, 2 * d:].T.astype(BF16))
    b_col = jnp.zeros((ROWS_BF16, 1), F32).at[:H_B, 0].set(b_f.astype(F32))
    w_qg = w_b_qg[0].astype(BF16)
    w_o = w_b_out[0].astype(BF16)
    row = lambda v: v.reshape(1, d).astype(F32)
    g_mix0, g_mix1 = row(norm_mix[0]), row(norm_mix[1])
    g_ffn0, g_ffn1 = row(norm_ffn[0]), row(norm_ffn[1])
    g_kv, g_fin, gn = row(norm_kv), row(norm_final), row(gn_ret[0])
    q_scale = LOG2E * dh ** -0.5

    def trunk_tokens(h1, m_tile):
        bb, tt, _ = h1.shape
        h2 = _ffn_block(h1.reshape(bb * tt, d), g_ffn0, w_up[0], w_down[0], tm=m_tile).reshape(bb, tt, d)
        return h2, _projections(h2, g_kv, g_mix1, w_kv, w_ft, b_col, w_qg, tm=m_tile,
                                q_scale=q_scale, c_scale=LOG2E)

    tc = _pick_tile(t, 512)
    chunk = _pick_tile(tc, 256)
    cos_p, sin_p = _rope_tables(jnp.arange(t), dk // 2)
    h1, s_prompt = _retention_layer(
        x_prompt, None, cos_p, sin_p, g_mix0, w_in, gn, w_out,
        _decay_tables(chunk, chunk, dk), perms, tc=tc, chunk=chunk)
    h2, (k_p, v_p, kb_p, vb_p, lf_p, c_p, q_p, gate_p) = trunk_tokens(h1, tc)
    a_p = _prompt_attention(q_p, kb_p, vb_p, c_p, gate_p, tq=_pick_tile(t, 512))
    y_prompt = _ffn_block(h2.reshape(b * t, d), g_ffn1, w_up[1], w_down[1], tm=tc,
                          attn=(a_p.reshape(b * t, d), w_o), g_final=g_fin).reshape(b, t, d)

    rows = LANES
    xs = jnp.zeros((bd, rows, d), x_sample.dtype).at[:, :td].set(x_sample)
    cos_s, sin_s = _rope_tables(past + jnp.arange(rows), dk // 2)
    h1s, s_sample = _retention_layer(
        xs, state_ret[0], cos_s, sin_s, g_mix0, w_in, gn, w_out,
        _decay_tables(rows, td, dk), perms, tc=rows, chunk=rows)
    h1s = h1s[:, :td].reshape(1, bd * td, d)
    h2s, (k_s, v_s, kb_s, vb_s, lf_s, _, q_s, gate_s) = trunk_tokens(h1s, bd * td)
    per_head = lambda a: a.reshape(bd * td * H_B, dh)
    lf_s_tok = jnp.swapaxes(lf_s[0], 0, 1)
    a_s = _sample_attention(
        page_table, per_head(q_s), per_head(gate_s), per_head(kb_s), per_head(vb_s),
        lf_s_tok.reshape(1, bd * td * H_B),
        cache_k.reshape(n_pool, page * H_B, dh), cache_v.reshape(n_pool, page * H_B, dh),
        cache_logf.reshape(n_pool, 1, page * H_B),
        group=8, t_new=td)
    y_sample = _ffn_block(h2s.reshape(bd * td, d), g_ffn1, w_up[1], w_down[1], tm=bd * td,
                          attn=(a_s.reshape(bd * td, d), w_o), g_final=g_fin).reshape(bd, td, d)

    return (y_prompt, y_sample, s_prompt[None],
            k_p.reshape(b, t, H_B, dh), v_p.reshape(b, t, H_B, dh), jnp.swapaxes(lf_p, 1, 2),
            s_sample[None],
            k_s.reshape(bd, td, H_B, dh), v_s.reshape(bd, td, H_B, dh),
            lf_s_tok.reshape(bd, td, H_B))
```

```python
import functools
import math

import jax
import jax.numpy as jnp
from jax import lax
from jax.experimental import pallas as pl
from jax.experimental.pallas import tpu as pltpu

H_A = 4
H_B = 8
RMS_EPS = 1e-6
GN_EPS = 1e-5
ROPE_BASE = 10000.0
LOG2E = math.log2(math.e)
LANES = 128
ROWS_BF16 = 16
VMEM_LIMIT_BYTES = 56 * 1024 * 1024

F32 = jnp.float32
BF16 = jnp.bfloat16


def _dot(a, b):
    return jnp.dot(a, b, preferred_element_type=F32)


def _dot_nt(a, b):
    return lax.dot_general(a, b, (((1,), (1,)), ((), ())), preferred_element_type=F32)


def _dot_tn(a, b):
    return lax.dot_general(a, b, (((0,), (0,)), ((), ())), preferred_element_type=F32)


def _rmsnorm(x, g):
    return x * lax.rsqrt(jnp.mean(x * x, axis=-1, keepdims=True) + RMS_EPS) * g


def _sigmoid(x):
    return 1.0 / (1.0 + jnp.exp(-x))


def _log_sigmoid(x):
    return jnp.minimum(x, 0.0) - jnp.log1p(jnp.exp(-jnp.abs(x)))


def _split3(x):
    return tuple(part.astype(BF16) for part in _split3_f32(x))


def _split3_f32(x):
    hi = x.astype(BF16).astype(F32)
    r1 = x - hi
    mid = r1.astype(BF16).astype(F32)
    lo = (r1 - mid).astype(BF16).astype(F32)
    return hi, mid, lo


def _const_spec(shape):
    nd = len(shape)
    return pl.BlockSpec(shape, lambda *_: (0,) * nd, pipeline_mode=pl.Buffered(1))


def _params(*sem):
    return pltpu.CompilerParams(dimension_semantics=sem, vmem_limit_bytes=VMEM_LIMIT_BYTES)


def _permute_rows(p_ref, s):
    p = p_ref[...]
    hi, mid, lo = _split3(s)
    return _dot(p, hi) + _dot(p, mid) + _dot(p, lo)


def _permute_cols(z, p_ref):
    p = p_ref[...]
    hi, mid, lo = _split3(z)
    return _dot(hi, p) + _dot(mid, p) + _dot(lo, p)


def _ret_kernel(*refs, chunk, has_state):
    refs = list(refs)
    x_ref = refs.pop(0)
    s0_ref = refs.pop(0) if has_state else None
    (cos_ref, sin_ref, g_ref, win_ref, gn_ref, wout_ref, intra_ref, qdec_ref, kdec_ref, cdec_ref,
     pcol_ref, prow_ref, h_ref, sfin_ref, s_scr, gate_scr) = refs
    t = pl.program_id(1)
    nseq, tc, d = x_ref.shape
    dk = d // H_A
    half = dk // 2

    @pl.when(t == 0)
    def _():
        for i in range(nseq):
            for h in range(H_A):
                s_scr[i * H_A + h] = s0_ref[i, h] if has_state else jnp.zeros((dk, dk), F32)

    x = x_ref[...].reshape(nseq * tc, d)
    hn = _rmsnorm(x, g_ref[...]).astype(BF16)
    cos = cos_ref[...]
    sin = sin_ref[...]

    def rot(z):
        z1, z2 = z[:, :half], z[:, half:]
        return jnp.concatenate([z1 * cos - z2 * sin, z1 * sin + z2 * cos], axis=1)

    for h in range(H_A):
        def proj(j, h=h):
            return _dot(hn, win_ref[:, j * d + h * dk:j * d + (h + 1) * dk])

        q = rot(proj(0)).astype(BF16)
        k = rot(proj(1)) * (dk ** -0.5)
        if has_state:
            q = _dot(q, pcol_ref[...]).astype(BF16)
            k = _permute_cols(k, pcol_ref)
        v = proj(2).astype(BF16)
        g = proj(3)
        for c in range(nseq * tc // chunk):
            r = slice(c * chunk, (c + 1) * chunk)
            st = (c * chunk // tc) * H_A + h
            qc, kc, vc = q[r], k[r], v[r]
            s_old = s_scr[st]
            scores = _dot_nt(qc, kc.astype(BF16)) * intra_ref[h]
            o = _dot(scores.astype(BF16), vc) + _dot(qc, s_old.astype(BF16)) * qdec_ref[h]
            s_scr[st] = s_old * cdec_ref[h] + _dot_tn((kc * kdec_ref[h]).astype(BF16), vc)
            mu = jnp.mean(o, axis=-1, keepdims=True)
            oc = o - mu
            var = jnp.mean(oc * oc, axis=-1, keepdims=True)
            on = oc * lax.rsqrt(var + GN_EPS) * gn_ref[:, h * dk:(h + 1) * dk]
            gc = g[r]
            gate_scr[r, h * dk:(h + 1) * dk] = (gc * _sigmoid(gc) * on).astype(BF16)

    h_ref[...] = (x + _dot(gate_scr[...], wout_ref[...])).reshape(nseq, tc, d)

    @pl.when(t == pl.num_programs(1) - 1)
    def _():
        for i in range(nseq):
            for h in range(H_A):
                s_fin = s_scr[i * H_A + h]
                sfin_ref[i, h] = s_fin if has_state else _permute_rows(prow_ref, s_fin)


def _retention_layer(x, s0, cos, sin, g, w_in, gn, w_out, dec, perms, *, tc, chunk, nseq=1):
    b, t, d = x.shape
    dk = d // H_A
    state_spec = pl.BlockSpec((nseq, H_A, dk, dk), lambda i, j: (i, 0, 0, 0))
    tok_spec = pl.BlockSpec((nseq, tc, d), lambda i, j: (i, j, 0))
    cos, sin = jnp.tile(cos, (nseq, 1)), jnp.tile(sin, (nseq, 1))
    rope_spec = pl.BlockSpec((nseq * tc, dk // 2), lambda i, j: (j, 0))
    consts = [g, w_in, gn, w_out, *dec, *perms]
    args = [x] + ([] if s0 is None else [s0]) + [cos, sin] + consts
    specs = ([tok_spec] + ([] if s0 is None else [state_spec]) + [rope_spec, rope_spec]
             + [_const_spec(a.shape) for a in consts])
    return pl.pallas_call(
        functools.partial(_ret_kernel, chunk=chunk, has_state=s0 is not None),
        name="retention_layer",
        grid=(b // nseq, t // tc),
        in_specs=specs,
        out_specs=[tok_spec, state_spec],
        out_shape=[jax.ShapeDtypeStruct((b, t, d), F32),
                   jax.ShapeDtypeStruct((b, H_A, dk, dk), F32)],
        scratch_shapes=[pltpu.VMEM((nseq * H_A, dk, dk), F32), pltpu.VMEM((nseq * tc, d), BF16)],
        compiler_params=_params("parallel", "arbitrary"),
    )(*args)


def _decay_tables(rows, c_true, dk):
    lg = jnp.log(1.0 - jnp.exp2(-5.0 - jnp.arange(H_A, dtype=F32)))
    i = jnp.arange(rows, dtype=F32)
    diff = i[:, None] - i[None, :]
    intra = jnp.where(diff >= 0, jnp.exp(lg[:, None, None] * jnp.maximum(diff, 0.0)), 0.0)
    qdec = jnp.exp(lg[:, None] * (i[None, :] + 1.0))
    kdec = jnp.exp(lg[:, None] * (c_true - 1.0 - i[None, :]))
    cdec = jnp.exp(lg * c_true)
    return (intra,
            jnp.broadcast_to(qdec[:, :, None], (H_A, rows, dk)),
            jnp.broadcast_to(kdec[:, :, None], (H_A, rows, dk)),
            jnp.broadcast_to(cdec[:, None, None], (H_A, 1, dk)))


def _rope_tables(pos, half):
    inv = 1.0 / (ROPE_BASE ** jnp.linspace(0.0, 1.0, half, dtype=F32))
    ang = pos.astype(F32)[:, None] * inv[None, :]
    return jnp.cos(ang), jnp.sin(ang)


def _ffn_kernel(*refs, ff, has_attn, final_norm):
    refs = list(refs)
    x_ref = refs.pop(0)
    if has_attn:
        a_ref, wo_ref = refs.pop(0), refs.pop(0)
    g_ref, wup_ref, wdown_ref = refs.pop(0), refs.pop(0), refs.pop(0)
    if final_norm:
        gfin_ref = refs.pop(0)
    o_ref, = refs

    x = x_ref[...]
    if has_attn:
        x = x + _dot(a_ref[...].astype(BF16), wo_ref[...])
    hn = _rmsnorm(x, g_ref[...]).astype(BF16)
    gate = _dot(hn, wup_ref[:, :ff])
    up = _dot(hn, wup_ref[:, ff:])
    act = (gate * _sigmoid(gate) * up).astype(BF16)
    y = x + _dot(act, wdown_ref[...])
    if final_norm:
        y = _rmsnorm(y, gfin_ref[...])
    o_ref[...] = y


def _layer_spec(w, layer):
    return pl.BlockSpec((None,) + w.shape[1:], lambda *_: (layer, 0, 0), pipeline_mode=pl.Buffered(1))


def _ffn_block(x, g, w_up, w_down, layer, *, tm, attn=None, g_final=None):
    m, d = x.shape
    ff = w_down.shape[1]
    row = lambda i: (i, 0)
    args, specs = [x], [pl.BlockSpec((tm, d), row)]
    if attn is not None:
        a, w_o = attn
        args += [a, w_o]
        specs += [pl.BlockSpec((tm, d), row), _const_spec(w_o.shape)]
    args += [g, w_up, w_down]
    specs += [_const_spec(g.shape), _layer_spec(w_up, layer), _layer_spec(w_down, layer)]
    if g_final is not None:
        args.append(g_final)
        specs.append(_const_spec(g_final.shape))
    return pl.pallas_call(
        functools.partial(_ffn_kernel, ff=ff, has_attn=attn is not None,
                          final_norm=g_final is not None),
        name="ffn_block",
        grid=(m // tm,),
        in_specs=specs,
        out_specs=pl.BlockSpec((tm, d), row),
        out_shape=jax.ShapeDtypeStruct((m, d), F32),
        compiler_params=_params("parallel"),
    )(*args)


def _proj_kernel(x_ref, gkv_ref, gq_ref, wkv_ref, wft_ref, bf_ref, tri_ref, wqg_ref,
                 k_ref, v_ref, kb_ref, vb_ref, lf_ref, c_ref, q_ref, gate_ref, carry_scr,
                 *, q_scale, c_scale):
    t = pl.program_id(1)
    d = x_ref.shape[2]
    tm = x_ref.shape[1]

    @pl.when(t == 0)
    def _():
        carry_scr[...] = jnp.zeros_like(carry_scr)

    x = x_ref[0]
    xn = x * lax.rsqrt(jnp.mean(x * x, axis=-1, keepdims=True) + RMS_EPS)
    hkv = (xn * gkv_ref[...]).astype(BF16)
    hq = (xn * gq_ref[...]).astype(BF16)

    kv = _dot(hkv, wkv_ref[...])
    k, v = kv[:, :d], kv[:, d:]
    k_ref[0] = k
    v_ref[0] = v
    kb_ref[0] = k.astype(BF16)
    vb_ref[0] = v.astype(BF16)

    qg = _dot(hq, wqg_ref[...])
    q_ref[0] = (qg[:, :d] * q_scale).astype(BF16)
    gate_ref[0] = qg[:, d:]

    lf = _log_sigmoid(_dot_nt(wft_ref[...], hkv) + bf_ref[...])
    lf_ref[0] = lf[:H_B]
    parts = _dot(jnp.concatenate(_split3(lf), axis=0), tri_ref[...])
    n = lf.shape[0]
    c = parts[:n] + parts[n:2 * n] + parts[2 * n:] + carry_scr[:, 0:1]
    c_ref[0] = c[:H_B] * c_scale
    carry_scr[...] = jnp.broadcast_to(c[:, tm - 1:tm], carry_scr.shape)


def _projections(x, g_kv, g_q, w_kv, w_ft, b_f, w_qg, *, tm, q_scale, c_scale):
    b, t, d = x.shape
    tri = (jnp.arange(tm)[:, None] <= jnp.arange(tm)[None, :]).astype(BF16)
    tok = lambda i, j: (i, j, 0)
    hm = lambda i, j: (i, 0, j)
    tok_spec = pl.BlockSpec((1, tm, d), tok)
    hm_spec = pl.BlockSpec((1, H_B, tm), hm)
    return pl.pallas_call(
        functools.partial(_proj_kernel, q_scale=q_scale, c_scale=c_scale),
        name="kv_qg_projections",
        grid=(b, t // tm),
        in_specs=[tok_spec, _const_spec(g_kv.shape), _const_spec(g_q.shape), _const_spec(w_kv.shape),
                  _const_spec(w_ft.shape), _const_spec(b_f.shape), _const_spec(tri.shape),
                  _const_spec(w_qg.shape)],
        out_specs=[tok_spec, tok_spec, tok_spec, tok_spec, hm_spec, hm_spec, tok_spec, tok_spec],
        out_shape=[jax.ShapeDtypeStruct((b, t, d), F32), jax.ShapeDtypeStruct((b, t, d), F32),
                   jax.ShapeDtypeStruct((b, t, d), BF16), jax.ShapeDtypeStruct((b, t, d), BF16),
                   jax.ShapeDtypeStruct((b, H_B, t), F32), jax.ShapeDtypeStruct((b, H_B, t), F32),
                   jax.ShapeDtypeStruct((b, t, d), BF16), jax.ShapeDtypeStruct((b, t, d), F32)],
        scratch_shapes=[pltpu.VMEM((ROWS_BF16, LANES), F32)],
        compiler_params=_params("parallel", "arbitrary"),
    )(x, g_kv, g_q, w_kv, w_ft, b_f, tri, w_qg)


def _attn_kernel(q_ref, k_ref, v_ref, c_ref, gate_ref, o_ref, *scr, dh):
    qi = pl.program_id(2)
    tq = q_ref.shape[1]
    lanes = [slice(h * dh, (h + 1) * dh) for h in range(q_ref.shape[2] // dh)]
    stats = [scr[3 * h:3 * h + 3] for h in range(len(lanes))]
    for m_scr, l_scr, acc_scr in stats:
        m_scr[...] = jnp.full_like(m_scr, -jnp.inf)
        l_scr[...] = jnp.zeros_like(l_scr)
        acc_scr[...] = jnp.zeros_like(acc_scr)

    def block(kj, diagonal):
        start = pl.multiple_of(kj * tq, tq)
        if diagonal:
            row = lax.broadcasted_iota(jnp.int32, (tq, tq), 0)
            col = lax.broadcasted_iota(jnp.int32, (tq, tq), 1)
            visible = col <= row
        for h, ln in enumerate(lanes):
            m_scr, l_scr, acc_scr = stats[h]
            ks = k_ref[0, pl.ds(start, tq), ln]
            vs = v_ref[0, pl.ds(start, tq), ln]
            s = _dot_nt(q_ref[0, :, ln], ks) - c_ref[0, h, pl.ds(kj, 1), :]
            if diagonal:
                s = jnp.where(visible, s, -jnp.inf)
            m_old = m_scr[...]
            m_new = jnp.maximum(m_old, jnp.max(s, axis=-1, keepdims=True))
            p = jnp.exp2(s - pltpu.repeat(m_new, tq // dh, axis=1))
            alpha = jnp.exp2(m_old - m_new)
            m_scr[...] = m_new
            l_scr[...] = alpha * l_scr[...] + jnp.sum(p, axis=-1, keepdims=True)
            acc_scr[...] = alpha * acc_scr[...] + _dot(p.astype(BF16), vs)

    def body(kj, _):
        block(kj, False)
        return 0

    lax.fori_loop(0, qi, body, 0)
    block(qi, True)
    for h, ln in enumerate(lanes):
        _, l_scr, acc_scr = stats[h]
        o_ref[0, :, ln] = (_sigmoid(gate_ref[0, :, ln]) * (acc_scr[...] / l_scr[...])).astype(o_ref.dtype)


def _prompt_attention(q, kb, vb, c, gate, *, tq, heads):
    b, s, d = q.shape
    dh = d // H_B
    nq = s // tq
    c4 = c.reshape(b, H_B, nq, tq)
    qspec = pl.BlockSpec((1, tq, heads * dh), lambda i, h, j: (i, j, h))
    kvspec = pl.BlockSpec((1, s, heads * dh), lambda i, h, j: (i, 0, h))
    return pl.pallas_call(
        functools.partial(_attn_kernel, dh=dh),
        name="prompt_attention",
        grid=(b, H_B // heads, nq),
        in_specs=[qspec, kvspec, kvspec,
                  pl.BlockSpec((1, heads, nq, tq), lambda i, h, j: (i, h, 0, 0)), qspec],
        out_specs=qspec,
        out_shape=jax.ShapeDtypeStruct((b, s, d), BF16),
        scratch_shapes=[pltpu.VMEM((tq, dh), F32)] * (3 * heads),
        compiler_params=_params("parallel", "parallel", "arbitrary"),
    )(q, kb, vb, c4, gate)


def _sample_attn_kernel(pt_ref, q_ref, gate_ref, knew_ref, vnew_ref, lfnew_ref, cum_ref, cumnew_ref,
                        *refs, group, t_new):
    k_refs = refs[:group]
    v_refs = refs[group:2 * group]
    lf_refs = refs[2 * group:3 * group]
    o_ref = refs[3 * group]
    m_scr, l_scr, acc_scr, carry_scr = refs[3 * group + 1:]
    b = pl.program_id(0)
    p = pl.program_id(1)
    nrow = q_ref.shape[0]
    ncol = k_refs[0].shape[0]
    stride = H_B
    q = q_ref[...]

    @pl.when(p == 0)
    def _():
        m_scr[...] = jnp.full_like(m_scr, -jnp.inf)
        l_scr[...] = jnp.zeros_like(l_scr)
        acc_scr[...] = jnp.zeros_like(acc_scr)
        carry_scr[...] = jnp.zeros_like(carry_scr)

    def same_head(n):
        row = lax.broadcasted_iota(jnp.int32, (nrow, n), 0)
        col = lax.broadcasted_iota(jnp.int32, (nrow, n), 1)
        return row % stride == col % stride, row, col

    def update(scores, values):
        m_old = m_scr[:, 0:1]
        m_new = m_old
        for s in scores:
            m_new = jnp.maximum(m_new, jnp.max(s, axis=-1, keepdims=True))
        alpha = jnp.exp2(m_old - m_new)
        l_new = alpha * l_scr[:, 0:1]
        acc = alpha * acc_scr[...]
        for s, v in zip(scores, values):
            pr = jnp.exp2(s - m_new)
            l_new = l_new + jnp.sum(pr, axis=-1, keepdims=True)
            acc = acc + _dot(pr.astype(BF16), v)
        l_scr[...] = jnp.broadcast_to(l_new, l_scr.shape)
        acc_scr[...] = acc
        m_scr[...] = jnp.broadcast_to(m_new, m_scr.shape)

    def running_sum(lf, cum_ref):
        n = lf.shape[0]
        parts = _dot(jnp.concatenate(_split3_f32(lf), axis=0).astype(BF16), cum_ref[...])
        return parts[:n] + parts[n:2 * n] + parts[2 * n:3 * n]

    lf = jnp.concatenate([r[...] for r in lf_refs], axis=0) * LOG2E
    within = running_sum(lf, cum_ref)
    mask, _, _ = same_head(ncol)
    carry = carry_scr[:, 0:1]
    scores, values = [], []
    for gi in range(group):
        w = within[gi * H_B:(gi + 1) * H_B]
        kb = k_refs[gi][...].astype(BF16)
        s = _dot_nt(q, kb) - jnp.tile(carry + w, (t_new, 1))
        scores.append(jnp.where(mask, s, -jnp.inf))
        values.append(v_refs[gi][...].astype(BF16))
        carry = carry + w[:, ncol - 1:ncol]
    carry_scr[...] = jnp.broadcast_to(carry, carry_scr.shape)
    update(scores, values)

    @pl.when(p == pl.num_programs(1) - 1)
    def _():
        n = knew_ref.shape[0]
        x = lfnew_ref[...] * LOG2E
        zeros = jnp.zeros_like(x)
        bias_new = carry + running_sum(jnp.concatenate([x, zeros], axis=0), cumnew_ref)[:H_B]
        mask_new, row, col = same_head(n)
        tok = col // stride
        visible = mask_new & (tok // t_new == b) & (tok % t_new <= row // stride)
        s_new = _dot_nt(q, knew_ref[...]) - jnp.tile(bias_new, (t_new, 1))
        update([jnp.where(visible, s_new, -jnp.inf)], [vnew_ref[...]])
        o_ref[...] = _sigmoid(gate_ref[...]) * (acc_scr[...] / l_scr[:, 0:1])


def _sample_attention(page_table, q, gate, k_new, v_new, lf_new, cache_k, cache_v, cache_lf, *, group, t_new):
    n_rows, dh = q.shape
    nrow = t_new * H_B
    bd = n_rows // nrow
    n_pages = page_table.shape[1]
    ncol = cache_k.shape[1]
    page = cache_lf.shape[2]
    n_tok = lf_new.shape[1]
    key_of = jnp.arange(ncol)[None, :] // H_B
    cum = (jnp.arange(page)[:, None] <= key_of).astype(BF16)
    tok, tok_of = jnp.arange(n_tok)[:, None], jnp.arange(n_tok * H_B)[None, :] // H_B
    cum_new = ((tok <= tok_of) & (tok // t_new == tok_of // t_new)).astype(BF16)
    per_b = pl.BlockSpec((nrow, dh), lambda i, j, pt: (i, 0))
    whole = lambda a: pl.BlockSpec(a.shape, lambda i, j, pt: (0,) * a.ndim)

    def paged(shape, gi):
        return pl.BlockSpec((None,) + shape, lambda i, j, pt: (pt[i, j * group + gi], 0, 0))

    in_specs = [per_b, per_b, whole(k_new), whole(v_new), whole(lf_new), whole(cum), whole(cum_new)]
    in_specs += [paged((ncol, dh), gi) for gi in range(group)]
    in_specs += [paged((ncol, dh), gi) for gi in range(group)]
    in_specs += [paged((H_B, page), gi) for gi in range(group)]
    grid_spec = pltpu.PrefetchScalarGridSpec(
        num_scalar_prefetch=1,
        grid=(bd, n_pages // group),
        in_specs=in_specs,
        out_specs=per_b,
        scratch_shapes=[pltpu.VMEM((nrow, LANES), F32), pltpu.VMEM((nrow, LANES), F32),
                        pltpu.VMEM((nrow, dh), F32), pltpu.VMEM((H_B, LANES), F32)],
    )
    return pl.pallas_call(
        functools.partial(_sample_attn_kernel, group=group, t_new=t_new),
        name="sample_attention",
        grid_spec=grid_spec,
        out_shape=jax.ShapeDtypeStruct((n_rows, dh), F32),
        compiler_params=_params("parallel", "arbitrary"),
    )(page_table, q, gate, k_new, v_new, lf_new, cum, cum_new,
      *([cache_k] * group), *([cache_v] * group), *([cache_lf] * group))


def _pick_tile(n, target):
    t = min(n, target)
    while n % t:
        t //= 2
    return t


def kernel(x_prompt, x_sample, state_ret, cache_k, cache_v, cache_logf, page_table,
           norm_mix, w_ret_in, gn_ret, w_ret_out, norm_kv, w_kvf, b_f,
           w_b_qg, w_b_out, norm_ffn, w_ffn_up, w_ffn_down, norm_final):
    b, t, d = x_prompt.shape
    bd, td, _ = x_sample.shape
    dk = d // H_A
    dh = d // H_B
    n_pool, page = cache_k.shape[0], cache_k.shape[1]
    past = page_table.shape[1] * page

    perm = jnp.concatenate([jnp.arange(0, dk, 2), jnp.arange(1, dk, 2)])
    head_cols = (jnp.arange(H_A)[:, None] * dk + perm[None, :]).reshape(-1)
    cols = jnp.concatenate([head_cols, d + head_cols, 2 * d + jnp.arange(2 * d)])
    w_in = w_ret_in[0][:, cols].astype(BF16)
    perm_in = (perm[:, None] == jnp.arange(dk)[None, :]).astype(BF16)
    perms = (perm_in, perm_in.T)
    w_out = w_ret_out[0].astype(BF16)
    w_up = w_ffn_up.astype(BF16)
    w_down = w_ffn_down.astype(BF16)
    w_kv = w_kvf[:, :2 * d].astype(BF16)
    w_ft = jnp.zeros((ROWS_BF16, d), BF16).at[:H_B].set(w_kvf[:, 2 * d:].T.astype(BF16))
    b_col = jnp.zeros((ROWS_BF16, 1), F32).at[:H_B, 0].set(b_f.astype(F32))
    w_qg = w_b_qg[0].astype(BF16)
    w_o = w_b_out[0].astype(BF16)
    row = lambda v: v.reshape(1, d).astype(F32)
    g_mix0, g_mix1 = row(norm_mix[0]), row(norm_mix[1])
    g_ffn0, g_ffn1 = row(norm_ffn[0]), row(norm_ffn[1])
    g_kv, g_fin, gn = row(norm_kv), row(norm_final), row(gn_ret[0])
    q_scale = LOG2E * dh ** -0.5

    def trunk_tokens(h1, m_tile):
        bb, tt, _ = h1.shape
        h2 = _ffn_block(h1.reshape(bb * tt, d), g_ffn0, w_up, w_down, 0, tm=m_tile).reshape(bb, tt, d)
        return h2, _projections(h2, g_kv, g_mix1, w_kv, w_ft, b_col, w_qg, tm=m_tile,
                                q_scale=q_scale, c_scale=LOG2E)

    tc = _pick_tile(t, 512)
    chunk = _pick_tile(tc, 256)
    cos_p, sin_p = _rope_tables(jnp.arange(t), dk // 2)
    h1, s_prompt = _retention_layer(
        x_prompt, None, cos_p, sin_p, g_mix0, w_in, gn, w_out,
        _decay_tables(chunk, chunk, dk), perms, tc=tc, chunk=chunk)
    h2, (k_p, v_p, kb_p, vb_p, lf_p, c_p, q_p, gate_p) = trunk_tokens(h1, tc)
    a_p = _prompt_attention(q_p, kb_p, vb_p, c_p, gate_p, tq=_pick_tile(t, 512), heads=4)
    y_prompt = _ffn_block(h2.reshape(b * t, d), g_ffn1, w_up, w_down, 1, tm=tc,
                          attn=(a_p.reshape(b * t, d), w_o), g_final=g_fin).reshape(b, t, d)

    rows = LANES
    xs = jnp.zeros((bd, rows, d), x_sample.dtype).at[:, :td].set(x_sample)
    cos_s, sin_s = _rope_tables(past + jnp.arange(rows), dk // 2)
    h1s, s_sample = _retention_layer(
        xs, state_ret[0], cos_s, sin_s, g_mix0, w_in, gn, w_out,
        _decay_tables(rows, td, dk), perms, tc=rows, chunk=rows, nseq=_pick_tile(bd, 4))
    h1s = h1s[:, :td].reshape(1, bd * td, d)
    h2s, (k_s, v_s, kb_s, vb_s, lf_s, _, q_s, gate_s) = trunk_tokens(h1s, bd * td)
    per_head = lambda a: a.reshape(bd * td * H_B, dh)
    lf_s_tok = jnp.swapaxes(lf_s[0], 0, 1)
    a_s = _sample_attention(
        page_table, per_head(q_s), per_head(gate_s), per_head(kb_s), per_head(vb_s),
        lf_s[0],
        cache_k.reshape(n_pool, page * H_B, dh), cache_v.reshape(n_pool, page * H_B, dh),
        jnp.swapaxes(cache_logf, 1, 2),
        group=8, t_new=td)
    y_sample = _ffn_block(h2s.reshape(bd * td, d), g_ffn1, w_up, w_down, 1, tm=bd * td,
                          attn=(a_s.reshape(bd * td, d), w_o), g_final=g_fin).reshape(bd, td, d)

    return (y_prompt, y_sample, s_prompt[None],
            k_p.reshape(b, t, H_B, dh), v_p.reshape(b, t, H_B, dh), jnp.swapaxes(lf_p, 1, 2),
            s_sample[None],
            k_s.reshape(bd, td, H_B, dh), v_s.reshape(bd, td, H_B, dh),
            lf_s_tok.reshape(bd, td, H_B))
```

```python
import functools
import math

import jax
import jax.numpy as jnp
from jax import lax
from jax.experimental import pallas as pl
from jax.experimental.pallas import tpu as pltpu

H_A = 4
H_B = 8
RMS_EPS = 1e-6
GN_EPS = 1e-5
ROPE_BASE = 10000.0
LOG2E = math.log2(math.e)
LANES = 128
ROWS_BF16 = 16
N_SLOTS = 3
VMEM_LIMIT_BYTES = 56 * 1024 * 1024

F32 = jnp.float32
BF16 = jnp.bfloat16


def _dot(a, b):
    return jnp.dot(a, b, preferred_element_type=F32)


def _dot_nt(a, b):
    return lax.dot_general(a, b, (((1,), (1,)), ((), ())), preferred_element_type=F32)


def _dot_tn(a, b):
    return lax.dot_general(a, b, (((0,), (0,)), ((), ())), preferred_element_type=F32)


def _rmsnorm(x, g):
    return x * lax.rsqrt(jnp.mean(x * x, axis=-1, keepdims=True) + RMS_EPS) * g


def _sigmoid(x):
    return 1.0 / (1.0 + jnp.exp(-x))


def _log_sigmoid(x):
    return jnp.minimum(x, 0.0) - jnp.log1p(jnp.exp(-jnp.abs(x)))


def _split3(x):
    return tuple(part.astype(BF16) for part in _split3_f32(x))


def _split3_f32(x):
    hi = x.astype(BF16).astype(F32)
    r1 = x - hi
    mid = r1.astype(BF16).astype(F32)
    lo = (r1 - mid).astype(BF16).astype(F32)
    return hi, mid, lo


def _const_spec(shape):
    nd = len(shape)
    return pl.BlockSpec(shape, lambda *_: (0,) * nd, pipeline_mode=pl.Buffered(1))


def _params(*sem):
    return pltpu.CompilerParams(dimension_semantics=sem, vmem_limit_bytes=VMEM_LIMIT_BYTES)


def _permute_rows(p_ref, s):
    p = p_ref[...]
    hi, mid, lo = _split3(s)
    return _dot(p, hi) + _dot(p, mid) + _dot(p, lo)


def _permute_cols(z, p_ref):
    p = p_ref[...]
    hi, mid, lo = _split3(z)
    return _dot(hi, p) + _dot(mid, p) + _dot(lo, p)


def _ret_kernel(*refs, chunk, has_state):
    refs = list(refs)
    x_ref = refs.pop(0)
    s0_ref = refs.pop(0) if has_state else None
    (cos_ref, sin_ref, g_ref, win_ref, gn_ref, wout_ref, intra_ref, qdec_ref, kdec_ref, cdec_ref,
     pcol_ref, prow_ref, h_ref, sfin_ref, s_scr, gate_scr) = refs
    t = pl.program_id(1)
    nseq, tc, d = x_ref.shape
    dk = d // H_A
    half = dk // 2

    @pl.when(t == 0)
    def _():
        for i in range(nseq):
            for h in range(H_A):
                s_scr[i * H_A + h] = s0_ref[i, h] if has_state else jnp.zeros((dk, dk), F32)

    x = x_ref[...].reshape(nseq * tc, d)
    hn = _rmsnorm(x, g_ref[...]).astype(BF16)
    cos = cos_ref[...]
    sin = sin_ref[...]

    def rot(z):
        z1, z2 = z[:, :half], z[:, half:]
        return jnp.concatenate([z1 * cos - z2 * sin, z1 * sin + z2 * cos], axis=1)

    for h in range(H_A):
        def proj(j, h=h):
            return _dot(hn, win_ref[:, j * d + h * dk:j * d + (h + 1) * dk])

        q = rot(proj(0)).astype(BF16)
        k = rot(proj(1)) * (dk ** -0.5)
        if has_state:
            q = _dot(q, pcol_ref[...]).astype(BF16)
            k = _permute_cols(k, pcol_ref)
        v = proj(2).astype(BF16)
        g = proj(3)
        for c in range(nseq * tc // chunk):
            r = slice(c * chunk, (c + 1) * chunk)
            st = (c * chunk // tc) * H_A + h
            qc, kc, vc = q[r], k[r], v[r]
            s_old = s_scr[st]
            scores = _dot_nt(qc, kc.astype(BF16)) * intra_ref[h]
            o = _dot(scores.astype(BF16), vc) + _dot(qc, s_old.astype(BF16)) * qdec_ref[h]
            s_scr[st] = s_old * cdec_ref[h] + _dot_tn((kc * kdec_ref[h]).astype(BF16), vc)
            mu = jnp.mean(o, axis=-1, keepdims=True)
            oc = o - mu
            var = jnp.mean(oc * oc, axis=-1, keepdims=True)
            on = oc * lax.rsqrt(var + GN_EPS) * gn_ref[:, h * dk:(h + 1) * dk]
            gc = g[r]
            gate_scr[r, h * dk:(h + 1) * dk] = (gc * _sigmoid(gc) * on).astype(BF16)

    h_ref[...] = (x + _dot(gate_scr[...], wout_ref[...])).reshape(nseq, tc, d)

    @pl.when(t == pl.num_programs(1) - 1)
    def _():
        for i in range(nseq):
            for h in range(H_A):
                s_fin = s_scr[i * H_A + h]
                sfin_ref[i, h] = s_fin if has_state else _permute_rows(prow_ref, s_fin)


def _retention_layer(x, s0, cos, sin, g, w_in, gn, w_out, dec, perms, *, tc, chunk, nseq=1):
    b, t, d = x.shape
    dk = d // H_A
    state_spec = pl.BlockSpec((nseq, H_A, dk, dk), lambda i, j: (i, 0, 0, 0))
    tok_spec = pl.BlockSpec((nseq, tc, d), lambda i, j: (i, j, 0))
    cos, sin = jnp.tile(cos, (nseq, 1)), jnp.tile(sin, (nseq, 1))
    rope_spec = pl.BlockSpec((nseq * tc, dk // 2), lambda i, j: (j, 0))
    consts = [g, w_in, gn, w_out, *dec, *perms]
    args = [x] + ([] if s0 is None else [s0]) + [cos, sin] + consts
    specs = ([tok_spec] + ([] if s0 is None else [state_spec]) + [rope_spec, rope_spec]
             + [_const_spec(a.shape) for a in consts])
    return pl.pallas_call(
        functools.partial(_ret_kernel, chunk=chunk, has_state=s0 is not None),
        name="retention_layer",
        grid=(b // nseq, t // tc),
        in_specs=specs,
        out_specs=[tok_spec, state_spec],
        out_shape=[jax.ShapeDtypeStruct((b, t, d), F32),
                   jax.ShapeDtypeStruct((b, H_A, dk, dk), F32)],
        scratch_shapes=[pltpu.VMEM((nseq * H_A, dk, dk), F32), pltpu.VMEM((nseq * tc, d), BF16)],
        compiler_params=_params("parallel", "arbitrary"),
    )(*args)


def _decay_tables(rows, c_true, dk):
    lg = jnp.log(1.0 - jnp.exp2(-5.0 - jnp.arange(H_A, dtype=F32)))
    i = jnp.arange(rows, dtype=F32)
    diff = i[:, None] - i[None, :]
    intra = jnp.where(diff >= 0, jnp.exp(lg[:, None, None] * jnp.maximum(diff, 0.0)), 0.0)
    qdec = jnp.exp(lg[:, None] * (i[None, :] + 1.0))
    kdec = jnp.exp(lg[:, None] * (c_true - 1.0 - i[None, :]))
    cdec = jnp.exp(lg * c_true)
    return (intra,
            jnp.broadcast_to(qdec[:, :, None], (H_A, rows, dk)),
            jnp.broadcast_to(kdec[:, :, None], (H_A, rows, dk)),
            jnp.broadcast_to(cdec[:, None, None], (H_A, 1, dk)))


def _rope_tables(pos, half):
    inv = 1.0 / (ROPE_BASE ** jnp.linspace(0.0, 1.0, half, dtype=F32))
    ang = pos.astype(F32)[:, None] * inv[None, :]
    return jnp.cos(ang), jnp.sin(ang)


def _ffn_kernel(*refs, ff, has_attn, final_norm):
    refs = list(refs)
    x_ref = refs.pop(0)
    if has_attn:
        a_ref, wo_ref = refs.pop(0), refs.pop(0)
    g_ref, wup_ref, wdown_ref = refs.pop(0), refs.pop(0), refs.pop(0)
    if final_norm:
        gfin_ref = refs.pop(0)
    o_ref, = refs

    x = x_ref[...]
    if has_attn:
        x = x + _dot(a_ref[...].astype(BF16), wo_ref[...])
    hn = _rmsnorm(x, g_ref[...]).astype(BF16)
    gate = _dot(hn, wup_ref[:, :ff])
    up = _dot(hn, wup_ref[:, ff:])
    act = (gate * _sigmoid(gate) * up).astype(BF16)
    y = x + _dot(act, wdown_ref[...])
    if final_norm:
        y = _rmsnorm(y, gfin_ref[...])
    o_ref[...] = y


def _layer_spec(w, layer):
    return pl.BlockSpec((None,) + w.shape[1:], lambda *_: (layer, 0, 0), pipeline_mode=pl.Buffered(1))


def _ffn_block(x, g, w_up, w_down, layer, *, tm, attn=None, g_final=None):
    m, d = x.shape
    ff = w_down.shape[1]
    row = lambda i: (i, 0)
    args, specs = [x], [pl.BlockSpec((tm, d), row)]
    if attn is not None:
        a, w_o = attn
        args += [a, w_o]
        specs += [pl.BlockSpec((tm, d), row), _const_spec(w_o.shape)]
    args += [g, w_up, w_down]
    specs += [_const_spec(g.shape), _layer_spec(w_up, layer), _layer_spec(w_down, layer)]
    if g_final is not None:
        args.append(g_final)
        specs.append(_const_spec(g_final.shape))
    return pl.pallas_call(
        functools.partial(_ffn_kernel, ff=ff, has_attn=attn is not None,
                          final_norm=g_final is not None),
        name="ffn_block",
        grid=(m // tm,),
        in_specs=specs,
        out_specs=pl.BlockSpec((tm, d), row),
        out_shape=jax.ShapeDtypeStruct((m, d), F32),
        compiler_params=_params("parallel"),
    )(*args)


def _proj_kernel(x_ref, gkv_ref, gq_ref, wkv_ref, wft_ref, bf_ref, tri_ref, wqg_ref,
                 k_ref, v_ref, kb_ref, vb_ref, lf_ref, c_ref, q_ref, gate_ref, carry_scr,
                 *, q_scale, c_scale):
    t = pl.program_id(1)
    d = x_ref.shape[2]
    tm = x_ref.shape[1]

    @pl.when(t == 0)
    def _():
        carry_scr[...] = jnp.zeros_like(carry_scr)

    x = x_ref[0]
    xn = x * lax.rsqrt(jnp.mean(x * x, axis=-1, keepdims=True) + RMS_EPS)
    hkv = (xn * gkv_ref[...]).astype(BF16)
    hq = (xn * gq_ref[...]).astype(BF16)

    kv = _dot(hkv, wkv_ref[...])
    k, v = kv[:, :d], kv[:, d:]
    k_ref[0] = k
    v_ref[0] = v
    kb_ref[0] = k.astype(BF16)
    vb_ref[0] = v.astype(BF16)

    qg = _dot(hq, wqg_ref[...])
    q_ref[0] = (qg[:, :d] * q_scale).astype(BF16)
    gate_ref[0] = qg[:, d:]

    lf = _log_sigmoid(_dot_nt(wft_ref[...], hkv) + bf_ref[...])
    lf_ref[0] = lf[:H_B]
    parts = _dot(jnp.concatenate(_split3(lf), axis=0), tri_ref[...])
    n = lf.shape[0]
    c = parts[:n] + parts[n:2 * n] + parts[2 * n:] + carry_scr[:, 0:1]
    c_ref[0] = c[:H_B] * c_scale
    carry_scr[...] = jnp.broadcast_to(c[:, tm - 1:tm], carry_scr.shape)


def _projections(x, g_kv, g_q, w_kv, w_ft, b_f, w_qg, *, tm, q_scale, c_scale):
    b, t, d = x.shape
    tri = (jnp.arange(tm)[:, None] <= jnp.arange(tm)[None, :]).astype(BF16)
    tok = lambda i, j: (i, j, 0)
    hm = lambda i, j: (i, 0, j)
    tok_spec = pl.BlockSpec((1, tm, d), tok)
    hm_spec = pl.BlockSpec((1, H_B, tm), hm)
    return pl.pallas_call(
        functools.partial(_proj_kernel, q_scale=q_scale, c_scale=c_scale),
        name="kv_qg_projections",
        grid=(b, t // tm),
        in_specs=[tok_spec, _const_spec(g_kv.shape), _const_spec(g_q.shape), _const_spec(w_kv.shape),
                  _const_spec(w_ft.shape), _const_spec(b_f.shape), _const_spec(tri.shape),
                  _const_spec(w_qg.shape)],
        out_specs=[tok_spec, tok_spec, tok_spec, tok_spec, hm_spec, hm_spec, tok_spec, tok_spec],
        out_shape=[jax.ShapeDtypeStruct((b, t, d), F32), jax.ShapeDtypeStruct((b, t, d), F32),
                   jax.ShapeDtypeStruct((b, t, d), BF16), jax.ShapeDtypeStruct((b, t, d), BF16),
                   jax.ShapeDtypeStruct((b, H_B, t), F32), jax.ShapeDtypeStruct((b, H_B, t), F32),
                   jax.ShapeDtypeStruct((b, t, d), BF16), jax.ShapeDtypeStruct((b, t, d), F32)],
        scratch_shapes=[pltpu.VMEM((ROWS_BF16, LANES), F32)],
        compiler_params=_params("parallel", "arbitrary"),
    )(x, g_kv, g_q, w_kv, w_ft, b_f, tri, w_qg)


def _attn_kernel(q_ref, k_ref, v_ref, c_ref, gate_ref, o_ref, *scr, dh):
    qi = pl.program_id(2)
    tq = q_ref.shape[1]
    lanes = [slice(h * dh, (h + 1) * dh) for h in range(q_ref.shape[2] // dh)]
    stats = [scr[3 * h:3 * h + 3] for h in range(len(lanes))]
    for m_scr, l_scr, acc_scr in stats:
        m_scr[...] = jnp.full_like(m_scr, -jnp.inf)
        l_scr[...] = jnp.zeros_like(l_scr)
        acc_scr[...] = jnp.zeros_like(acc_scr)

    def block(kj, diagonal):
        start = pl.multiple_of(kj * tq, tq)
        if diagonal:
            row = lax.broadcasted_iota(jnp.int32, (tq, tq), 0)
            col = lax.broadcasted_iota(jnp.int32, (tq, tq), 1)
            visible = col <= row
        for h, ln in enumerate(lanes):
            m_scr, l_scr, acc_scr = stats[h]
            ks = k_ref[0, pl.ds(start, tq), ln]
            vs = v_ref[0, pl.ds(start, tq), ln]
            s = _dot_nt(q_ref[0, :, ln], ks) - c_ref[0, h, pl.ds(kj, 1), :]
            if diagonal:
                s = jnp.where(visible, s, -jnp.inf)
            chunks = [s[:, j * dh:(j + 1) * dh] for j in range(tq // dh)]
            m_old = m_scr[...]
            m_new = jnp.maximum(m_old, jnp.max(functools.reduce(jnp.maximum, chunks),
                                               axis=-1, keepdims=True))
            parts = [jnp.exp2(c - m_new) for c in chunks]
            alpha = jnp.exp2(m_old - m_new)
            m_scr[...] = m_new
            l_scr[...] = alpha * l_scr[...] + jnp.sum(functools.reduce(jnp.add, parts),
                                                      axis=-1, keepdims=True)
            p = jnp.concatenate([part.astype(BF16) for part in parts], axis=1)
            acc_scr[...] = alpha * acc_scr[...] + _dot(p, vs)

    def body(kj, _):
        block(kj, False)
        return 0

    lax.fori_loop(0, qi, body, 0)
    block(qi, True)
    for h, ln in enumerate(lanes):
        _, l_scr, acc_scr = stats[h]
        o_ref[0, :, ln] = (_sigmoid(gate_ref[0, :, ln]) * (acc_scr[...] / l_scr[...])).astype(o_ref.dtype)


def _prompt_attention(q, kb, vb, c, gate, *, tq, heads):
    b, s, d = q.shape
    dh = d // H_B
    nq = s // tq
    c4 = c.reshape(b, H_B, nq, tq)
    qspec = pl.BlockSpec((1, tq, heads * dh), lambda i, h, j: (i, j, h))
    kvspec = pl.BlockSpec((1, s, heads * dh), lambda i, h, j: (i, 0, h))
    return pl.pallas_call(
        functools.partial(_attn_kernel, dh=dh),
        name="prompt_attention",
        grid=(b, H_B // heads, nq),
        in_specs=[qspec, kvspec, kvspec,
                  pl.BlockSpec((1, heads, nq, tq), lambda i, h, j: (i, h, 0, 0)), qspec],
        out_specs=qspec,
        out_shape=jax.ShapeDtypeStruct((b, s, d), BF16),
        scratch_shapes=[pltpu.VMEM((tq, dh), F32)] * (3 * heads),
        compiler_params=_params("parallel", "parallel", "arbitrary"),
    )(q, kb, vb, c4, gate)


def _sample_attn_kernel(pt_ref, q_ref, gate_ref, knew_ref, vnew_ref, lfnew_ref, cum_ref, cumnew_ref,
                        k_hbm, v_hbm, lf_hbm, o_ref, m_scr, l_scr, acc_scr, carry_scr,
                        k_buf, v_buf, lf_buf, k_sem, v_sem, lf_sem, *, group, t_new):
    b = pl.program_id(0)
    p = pl.program_id(1)
    n_steps = pl.num_programs(1)
    nrow = q_ref.shape[0]
    ncol = k_hbm.shape[1]
    stride = H_B
    q = q_ref[...]

    step = b * n_steps + p
    total_steps = pl.num_programs(0) * n_steps

    def page_copies(step_idx, slot):
        bb = step_idx // n_steps
        first = (step_idx % n_steps) * group
        copies = []
        for gi in range(group):
            page = pt_ref[bb, first + gi]
            dst = slot * group + gi
            copies += [pltpu.make_async_copy(k_hbm.at[page], k_buf.at[dst], k_sem.at[slot]),
                       pltpu.make_async_copy(v_hbm.at[page], v_buf.at[dst], v_sem.at[slot]),
                       pltpu.make_async_copy(lf_hbm.at[page], lf_buf.at[dst], lf_sem.at[slot])]
        return copies

    @pl.when(step == 0)
    def _():
        for ahead in range(N_SLOTS - 1):
            for cp in page_copies(ahead, ahead):
                cp.start()

    @pl.when(step + (N_SLOTS - 1) < total_steps)
    def _():
        nxt = step + (N_SLOTS - 1)
        for cp in page_copies(nxt, nxt % N_SLOTS):
            cp.start()

    slot = step % N_SLOTS
    for cp in page_copies(step, slot):
        cp.wait()
    k_refs = [k_buf.at[slot * group + gi] for gi in range(group)]
    v_refs = [v_buf.at[slot * group + gi] for gi in range(group)]
    lf_refs = [lf_buf.at[slot * group + gi] for gi in range(group)]

    @pl.when(p == 0)
    def _():
        m_scr[...] = jnp.full_like(m_scr, -jnp.inf)
        l_scr[...] = jnp.zeros_like(l_scr)
        acc_scr[...] = jnp.zeros_like(acc_scr)
        carry_scr[...] = jnp.zeros_like(carry_scr)

    def same_head(n):
        row = lax.broadcasted_iota(jnp.int32, (nrow, n), 0)
        col = lax.broadcasted_iota(jnp.int32, (nrow, n), 1)
        return row % stride == col % stride, row, col

    def update(scores, values):
        m_old = m_scr[:, 0:1]
        m_new = m_old
        for s in scores:
            m_new = jnp.maximum(m_new, jnp.max(s, axis=-1, keepdims=True))
        alpha = jnp.exp2(m_old - m_new)
        l_new = alpha * l_scr[:, 0:1]
        acc = alpha * acc_scr[...]
        for s, v in zip(scores, values):
            pr = jnp.exp2(s - m_new)
            l_new = l_new + jnp.sum(pr, axis=-1, keepdims=True)
            acc = acc + _dot(pr.astype(BF16), v)
        l_scr[...] = jnp.broadcast_to(l_new, l_scr.shape)
        acc_scr[...] = acc
        m_scr[...] = jnp.broadcast_to(m_new, m_scr.shape)

    def running_sum(lf, cum_ref):
        n = lf.shape[0]
        parts = _dot(jnp.concatenate(_split3_f32(lf), axis=0).astype(BF16), cum_ref[...])
        return parts[:n] + parts[n:2 * n] + parts[2 * n:3 * n]

    lf = jnp.concatenate([r[...] for r in lf_refs], axis=0) * LOG2E
    within = running_sum(lf, cum_ref)
    mask, _, _ = same_head(ncol)
    carry = carry_scr[:, 0:1]
    scores, values = [], []
    for gi in range(group):
        w = within[gi * H_B:(gi + 1) * H_B]
        kb = k_refs[gi][...].astype(BF16)
        s = _dot_nt(q, kb) - jnp.tile(carry + w, (t_new, 1))
        scores.append(jnp.where(mask, s, -jnp.inf))
        values.append(v_refs[gi][...].astype(BF16))
        carry = carry + w[:, ncol - 1:ncol]
    carry_scr[...] = jnp.broadcast_to(carry, carry_scr.shape)
    update(scores, values)

    @pl.when(p == pl.num_programs(1) - 1)
    def _():
        n = knew_ref.shape[0]
        x = lfnew_ref[...] * LOG2E
        zeros = jnp.zeros_like(x)
        bias_new = carry + running_sum(jnp.concatenate([x, zeros], axis=0), cumnew_ref)[:H_B]
        mask_new, row, col = same_head(n)
        tok = col // stride
        visible = mask_new & (tok // t_new == b) & (tok % t_new <= row // stride)
        s_new = _dot_nt(q, knew_ref[...]) - jnp.tile(bias_new, (t_new, 1))
        update([jnp.where(visible, s_new, -jnp.inf)], [vnew_ref[...]])
        o_ref[...] = _sigmoid(gate_ref[...]) * (acc_scr[...] / l_scr[:, 0:1])


def _sample_attention(page_table, q, gate, k_new, v_new, lf_new, cache_k, cache_v, cache_lf, *, group, t_new):
    n_rows, dh = q.shape
    nrow = t_new * H_B
    bd = n_rows // nrow
    n_pages = page_table.shape[1]
    ncol = cache_k.shape[1]
    page = cache_lf.shape[2]
    n_tok = lf_new.shape[1]
    key_of = jnp.arange(ncol)[None, :] // H_B
    cum = (jnp.arange(page)[:, None] <= key_of).astype(BF16)
    tok, tok_of = jnp.arange(n_tok)[:, None], jnp.arange(n_tok * H_B)[None, :] // H_B
    cum_new = ((tok <= tok_of) & (tok // t_new == tok_of // t_new)).astype(BF16)
    per_b = pl.BlockSpec((nrow, dh), lambda i, j, pt: (i, 0))
    whole = lambda a: pl.BlockSpec(a.shape, lambda i, j, pt: (0,) * a.ndim)

    hbm = pl.BlockSpec(memory_space=pl.ANY)
    in_specs = [per_b, per_b, whole(k_new), whole(v_new), whole(lf_new), whole(cum), whole(cum_new),
                hbm, hbm, hbm]
    n_buf = N_SLOTS * group
    grid_spec = pltpu.PrefetchScalarGridSpec(
        num_scalar_prefetch=1,
        grid=(bd, n_pages // group),
        in_specs=in_specs,
        out_specs=per_b,
        scratch_shapes=[pltpu.VMEM((nrow, LANES), F32), pltpu.VMEM((nrow, LANES), F32),
                        pltpu.VMEM((nrow, dh), F32), pltpu.VMEM((H_B, LANES), F32),
                        pltpu.VMEM((n_buf, ncol, dh), cache_k.dtype),
                        pltpu.VMEM((n_buf, ncol, dh), cache_v.dtype),
                        pltpu.VMEM((n_buf, H_B, page), cache_lf.dtype),
                        pltpu.SemaphoreType.DMA((N_SLOTS,)), pltpu.SemaphoreType.DMA((N_SLOTS,)),
                        pltpu.SemaphoreType.DMA((N_SLOTS,))],
    )
    return pl.pallas_call(
        functools.partial(_sample_attn_kernel, group=group, t_new=t_new),
        name="sample_attention",
        grid_spec=grid_spec,
        out_shape=jax.ShapeDtypeStruct((n_rows, dh), F32),
        compiler_params=_params("arbitrary", "arbitrary"),
    )(page_table, q, gate, k_new, v_new, lf_new, cum, cum_new, cache_k, cache_v, cache_lf)


def _pick_tile(n, target):
    t = min(n, target)
    while n % t:
        t //= 2
    return t


def kernel(x_prompt, x_sample, state_ret, cache_k, cache_v, cache_logf, page_table,
           norm_mix, w_ret_in, gn_ret, w_ret_out, norm_kv, w_kvf, b_f,
           w_b_qg, w_b_out, norm_ffn, w_ffn_up, w_ffn_down, norm_final):
    b, t, d = x_prompt.shape
    bd, td, _ = x_sample.shape
    dk = d // H_A
    dh = d // H_B
    n_pool, page = cache_k.shape[0], cache_k.shape[1]
    past = page_table.shape[1] * page

    perm = jnp.concatenate([jnp.arange(0, dk, 2), jnp.arange(1, dk, 2)])
    head_cols = (jnp.arange(H_A)[:, None] * dk + perm[None, :]).reshape(-1)
    cols = jnp.concatenate([head_cols, d + head_cols, 2 * d + jnp.arange(2 * d)])
    w_in = w_ret_in[0][:, cols].astype(BF16)
    perm_in = (perm[:, None] == jnp.arange(dk)[None, :]).astype(BF16)
    perms = (perm_in, perm_in.T)
    w_out = w_ret_out[0].astype(BF16)
    w_up = w_ffn_up.astype(BF16)
    w_down = w_ffn_down.astype(BF16)
    w_kv = w_kvf[:, :2 * d].astype(BF16)
    w_ft = jnp.zeros((ROWS_BF16, d), BF16).at[:H_B].set(w_kvf[:, 2 * d:].T.astype(BF16))
    b_col = jnp.zeros((ROWS_BF16, 1), F32).at[:H_B, 0].set(b_f.astype(F32))
    w_qg = w_b_qg[0].astype(BF16)
    w_o = w_b_out[0].astype(BF16)
    row = lambda v: v.reshape(1, d).astype(F32)
    g_mix0, g_mix1 = row(norm_mix[0]), row(norm_mix[1])
    g_ffn0, g_ffn1 = row(norm_ffn[0]), row(norm_ffn[1])
    g_kv, g_fin, gn = row(norm_kv), row(norm_final), row(gn_ret[0])
    q_scale = LOG2E * dh ** -0.5

    def trunk_tokens(h1, m_tile):
        bb, tt, _ = h1.shape
        h2 = _ffn_block(h1.reshape(bb * tt, d), g_ffn0, w_up, w_down, 0, tm=m_tile).reshape(bb, tt, d)
        return h2, _projections(h2, g_kv, g_mix1, w_kv, w_ft, b_col, w_qg, tm=m_tile,
                                q_scale=q_scale, c_scale=LOG2E)

    tc = _pick_tile(t, 512)
    chunk = _pick_tile(tc, 256)
    cos_p, sin_p = _rope_tables(jnp.arange(t), dk // 2)
    h1, s_prompt = _retention_layer(
        x_prompt, None, cos_p, sin_p, g_mix0, w_in, gn, w_out,
        _decay_tables(chunk, chunk, dk), perms, tc=tc, chunk=chunk)
    h2, (k_p, v_p, kb_p, vb_p, lf_p, c_p, q_p, gate_p) = trunk_tokens(h1, tc)
    a_p = _prompt_attention(q_p, kb_p, vb_p, c_p, gate_p, tq=_pick_tile(t, 512), heads=4)
    y_prompt = _ffn_block(h2.reshape(b * t, d), g_ffn1, w_up, w_down, 1, tm=tc,
                          attn=(a_p.reshape(b * t, d), w_o), g_final=g_fin).reshape(b, t, d)

    rows = LANES
    xs = jnp.zeros((bd, rows, d), x_sample.dtype).at[:, :td].set(x_sample)
    cos_s, sin_s = _rope_tables(past + jnp.arange(rows), dk // 2)
    h1s, s_sample = _retention_layer(
        xs, state_ret[0], cos_s, sin_s, g_mix0, w_in, gn, w_out,
        _decay_tables(rows, td, dk), perms, tc=rows, chunk=rows, nseq=_pick_tile(bd, 4))
    h1s = h1s[:, :td].reshape(1, bd * td, d)
    h2s, (k_s, v_s, kb_s, vb_s, lf_s, _, q_s, gate_s) = trunk_tokens(h1s, bd * td)
    per_head = lambda a: a.reshape(bd * td * H_B, dh)
    lf_s_tok = jnp.swapaxes(lf_s[0], 0, 1)
    a_s = _sample_attention(
        page_table, per_head(q_s), per_head(gate_s), per_head(kb_s), per_head(vb_s),
        lf_s[0],
        cache_k.reshape(n_pool, page * H_B, dh), cache_v.reshape(n_pool, page * H_B, dh),
        jnp.swapaxes(cache_logf, 1, 2),
        group=8, t_new=td)
    y_sample = _ffn_block(h2s.reshape(bd * td, d), g_ffn1, w_up, w_down, 1, tm=bd * td,
                          attn=(a_s.reshape(bd * td, d), w_o), g_final=g_fin).reshape(bd, td, d)

    return (y_prompt, y_sample, s_prompt[None],
            k_p.reshape(b, t, H_B, dh), v_p.reshape(b, t, H_B, dh), jnp.swapaxes(lf_p, 1, 2),
            s_sample[None],
            k_s.reshape(bd, td, H_B, dh), v_s.reshape(bd, td, H_B, dh),
            lf_s_tok.reshape(bd, td, H_B))
```

```python
import functools
import math

import jax
import jax.numpy as jnp
from jax import lax
from jax.experimental import pallas as pl
from jax.experimental.pallas import tpu as pltpu

H_A = 4
H_B = 8
RMS_EPS = 1e-6
GN_EPS = 1e-5
ROPE_BASE = 10000.0
LOG2E = math.log2(math.e)
LANES = 128
ROWS_BF16 = 16
N_SLOTS = 3
VMEM_LIMIT_BYTES = 56 * 1024 * 1024

F32 = jnp.float32
BF16 = jnp.bfloat16


def _dot(a, b):
    return jnp.dot(a, b, preferred_element_type=F32)


def _dot_nt(a, b):
    return lax.dot_general(a, b, (((1,), (1,)), ((), ())), preferred_element_type=F32)


def _dot_tn(a, b):
    return lax.dot_general(a, b, (((0,), (0,)), ((), ())), preferred_element_type=F32)


def _rmsnorm(x, g):
    return x * lax.rsqrt(jnp.mean(x * x, axis=-1, keepdims=True) + RMS_EPS) * g


def _sigmoid(x):
    return 1.0 / (1.0 + jnp.exp(-x))


def _log_sigmoid(x):
    return jnp.minimum(x, 0.0) - jnp.log1p(jnp.exp(-jnp.abs(x)))


def _split3(x):
    return tuple(part.astype(BF16) for part in _split3_f32(x))


def _split3_f32(x):
    hi = x.astype(BF16).astype(F32)
    r1 = x - hi
    mid = r1.astype(BF16).astype(F32)
    lo = (r1 - mid).astype(BF16).astype(F32)
    return hi, mid, lo


def _const_spec(shape):
    nd = len(shape)
    return pl.BlockSpec(shape, lambda *_: (0,) * nd, pipeline_mode=pl.Buffered(1))


def _params(*sem):
    return pltpu.CompilerParams(dimension_semantics=sem, vmem_limit_bytes=VMEM_LIMIT_BYTES)


def _permute_rows(p_ref, s):
    p = p_ref[...]
    hi, mid, lo = _split3(s)
    return _dot(p, hi) + _dot(p, mid) + _dot(p, lo)


def _permute_cols(z, p_ref):
    p = p_ref[...]
    hi, mid, lo = _split3(z)
    return _dot(hi, p) + _dot(mid, p) + _dot(lo, p)


def _ret_kernel(*refs, chunk, has_state):
    refs = list(refs)
    x_ref = refs.pop(0)
    s0_ref = refs.pop(0) if has_state else None
    (cos_ref, sin_ref, g_ref, win_ref, gn_ref, wout_ref, intra_ref, qdec_ref, kdec_ref, cdec_ref,
     pcol_ref, prow_ref, h_ref, sfin_ref, s_scr, gate_scr, *pad_scr) = refs
    t = pl.program_id(1)
    nseq, tx, d = x_ref.shape
    tc = gate_scr.shape[0] // nseq
    dk = d // H_A
    half = dk // 2

    @pl.when(t == 0)
    def _():
        for i in range(nseq):
            for h in range(H_A):
                s_scr[i * H_A + h] = s0_ref[i, h] if has_state else jnp.zeros((dk, dk), F32)

    if tx == tc:
        x = x_ref[...].reshape(nseq * tc, d)
    else:
        xpad_scr, = pad_scr
        xpad_scr[...] = jnp.zeros_like(xpad_scr)
        for i in range(nseq):
            xpad_scr[i * tc:i * tc + tx, :] = x_ref[i]
        x = xpad_scr[...]
    hn = _rmsnorm(x, g_ref[...]).astype(BF16)
    cos = cos_ref[...]
    sin = sin_ref[...]

    def rot(z):
        z1, z2 = z[:, :half], z[:, half:]
        return jnp.concatenate([z1 * cos - z2 * sin, z1 * sin + z2 * cos], axis=1)

    for h in range(H_A):
        def proj(j, h=h):
            return _dot(hn, win_ref[:, j * d + h * dk:j * d + (h + 1) * dk])

        q = rot(proj(0)).astype(BF16)
        k = rot(proj(1)) * (dk ** -0.5)
        if has_state:
            q = _dot(q, pcol_ref[...]).astype(BF16)
            k = _permute_cols(k, pcol_ref)
        v = proj(2).astype(BF16)
        g = proj(3)
        for c in range(nseq * tc // chunk):
            r = slice(c * chunk, (c + 1) * chunk)
            st = (c * chunk // tc) * H_A + h
            qc, kc, vc = q[r], k[r], v[r]
            s_old = s_scr[st]
            scores = _dot_nt(qc, kc.astype(BF16)) * intra_ref[h]
            o = _dot(scores.astype(BF16), vc) + _dot(qc, s_old.astype(BF16)) * qdec_ref[h]
            s_scr[st] = s_old * cdec_ref[h] + _dot_tn((kc * kdec_ref[h]).astype(BF16), vc)
            mu = jnp.mean(o, axis=-1, keepdims=True)
            oc = o - mu
            var = jnp.mean(oc * oc, axis=-1, keepdims=True)
            on = oc * lax.rsqrt(var + GN_EPS) * gn_ref[:, h * dk:(h + 1) * dk]
            gc = g[r]
            gate_scr[r, h * dk:(h + 1) * dk] = (gc * _sigmoid(gc) * on).astype(BF16)

    out = x + _dot(gate_scr[...], wout_ref[...])
    if tx == tc:
        h_ref[...] = out.reshape(nseq, tc, d)
    else:
        for i in range(nseq):
            h_ref[i] = out[i * tc:i * tc + tx]

    @pl.when(t == pl.num_programs(1) - 1)
    def _():
        for i in range(nseq):
            for h in range(H_A):
                s_fin = s_scr[i * H_A + h]
                sfin_ref[i, h] = s_fin if has_state else _permute_rows(prow_ref, s_fin)


def _retention_layer(x, s0, cos, sin, g, w_in, gn, w_out, dec, perms, *, tc, chunk, nseq=1):
    b, t, d = x.shape
    dk = d // H_A
    tx = min(t, tc)
    state_spec = pl.BlockSpec((nseq, H_A, dk, dk), lambda i, j: (i, 0, 0, 0))
    tok_spec = pl.BlockSpec((nseq, tx, d), lambda i, j: (i, j, 0))
    pad_scratch = [] if tx == tc else [pltpu.VMEM((nseq * tc, d), F32)]
    cos, sin = jnp.tile(cos, (nseq, 1)), jnp.tile(sin, (nseq, 1))
    rope_spec = pl.BlockSpec((nseq * tc, dk // 2), lambda i, j: (j, 0))
    consts = [g, w_in, gn, w_out, *dec, *perms]
    args = [x] + ([] if s0 is None else [s0]) + [cos, sin] + consts
    specs = ([tok_spec] + ([] if s0 is None else [state_spec]) + [rope_spec, rope_spec]
             + [_const_spec(a.shape) for a in consts])
    return pl.pallas_call(
        functools.partial(_ret_kernel, chunk=chunk, has_state=s0 is not None),
        name="retention_layer",
        grid=(b // nseq, t // tx),
        in_specs=specs,
        out_specs=[tok_spec, state_spec],
        out_shape=[jax.ShapeDtypeStruct((b, t, d), F32),
                   jax.ShapeDtypeStruct((b, H_A, dk, dk), F32)],
        scratch_shapes=[pltpu.VMEM((nseq * H_A, dk, dk), F32), pltpu.VMEM((nseq * tc, d), BF16),
                        *pad_scratch],
        compiler_params=_params("parallel", "arbitrary"),
    )(*args)


def _decay_tables(rows, c_true, dk):
    lg = jnp.log(1.0 - jnp.exp2(-5.0 - jnp.arange(H_A, dtype=F32)))
    i = jnp.arange(rows, dtype=F32)
    diff = i[:, None] - i[None, :]
    intra = jnp.where(diff >= 0, jnp.exp(lg[:, None, None] * jnp.maximum(diff, 0.0)), 0.0)
    qdec = jnp.exp(lg[:, None] * (i[None, :] + 1.0))
    kdec = jnp.exp(lg[:, None] * (c_true - 1.0 - i[None, :]))
    cdec = jnp.exp(lg * c_true)
    return (intra,
            jnp.broadcast_to(qdec[:, :, None], (H_A, rows, dk)),
            jnp.broadcast_to(kdec[:, :, None], (H_A, rows, dk)),
            jnp.broadcast_to(cdec[:, None, None], (H_A, 1, dk)))


def _rope_tables(pos, half):
    inv = 1.0 / (ROPE_BASE ** jnp.linspace(0.0, 1.0, half, dtype=F32))
    ang = pos.astype(F32)[:, None] * inv[None, :]
    return jnp.cos(ang), jnp.sin(ang)


def _ffn_kernel(*refs, ff, has_attn, final_norm):
    refs = list(refs)
    x_ref = refs.pop(0)
    if has_attn:
        a_ref, wo_ref = refs.pop(0), refs.pop(0)
    g_ref, wup_ref, wdown_ref = refs.pop(0), refs.pop(0), refs.pop(0)
    if final_norm:
        gfin_ref = refs.pop(0)
    o_ref, = refs

    x = x_ref[...]
    if has_attn:
        x = x + _dot(a_ref[...].astype(BF16), wo_ref[...])
    hn = _rmsnorm(x, g_ref[...]).astype(BF16)
    gate = _dot(hn, wup_ref[:, :ff])
    up = _dot(hn, wup_ref[:, ff:])
    act = (gate * _sigmoid(gate) * up).astype(BF16)
    y = x + _dot(act, wdown_ref[...])
    if final_norm:
        y = _rmsnorm(y, gfin_ref[...])
    o_ref[...] = y


def _layer_spec(w, layer):
    return pl.BlockSpec((None,) + w.shape[1:], lambda *_: (layer, 0, 0), pipeline_mode=pl.Buffered(1))


def _ffn_block(x, g, w_up, w_down, layer, *, tm, attn=None, g_final=None):
    m, d = x.shape
    ff = w_down.shape[1]
    row = lambda i: (i, 0)
    args, specs = [x], [pl.BlockSpec((tm, d), row)]
    if attn is not None:
        a, w_o = attn
        args += [a, w_o]
        specs += [pl.BlockSpec((tm, d), row), _const_spec(w_o.shape)]
    args += [g, w_up, w_down]
    specs += [_const_spec(g.shape), _layer_spec(w_up, layer), _layer_spec(w_down, layer)]
    if g_final is not None:
        args.append(g_final)
        specs.append(_const_spec(g_final.shape))
    return pl.pallas_call(
        functools.partial(_ffn_kernel, ff=ff, has_attn=attn is not None,
                          final_norm=g_final is not None),
        name="ffn_block",
        grid=(m // tm,),
        in_specs=specs,
        out_specs=pl.BlockSpec((tm, d), row),
        out_shape=jax.ShapeDtypeStruct((m, d), F32),
        compiler_params=_params("parallel"),
    )(*args)


def _proj_kernel(x_ref, gkv_ref, gq_ref, wkv_ref, wft_ref, bf_ref, tri_ref, wqg_ref,
                 k_ref, v_ref, kb_ref, vb_ref, lf_ref, c_ref, q_ref, gate_ref, carry_scr,
                 *, q_scale, c_scale):
    t = pl.program_id(1)
    d = x_ref.shape[2]
    tm = x_ref.shape[1]

    @pl.when(t == 0)
    def _():
        carry_scr[...] = jnp.zeros_like(carry_scr)

    x = x_ref[0]
    xn = x * lax.rsqrt(jnp.mean(x * x, axis=-1, keepdims=True) + RMS_EPS)
    hkv = (xn * gkv_ref[...]).astype(BF16)
    hq = (xn * gq_ref[...]).astype(BF16)

    kv = _dot(hkv, wkv_ref[...])
    k, v = kv[:, :d], kv[:, d:]
    k_ref[0] = k
    v_ref[0] = v
    kb_ref[0] = k.astype(BF16)
    vb_ref[0] = v.astype(BF16)

    qg = _dot(hq, wqg_ref[...])
    q_ref[0] = (qg[:, :d] * q_scale).astype(BF16)
    gate_ref[0] = qg[:, d:]

    lf = _log_sigmoid(_dot_nt(wft_ref[...], hkv) + bf_ref[...])
    lf_ref[0] = lf[:H_B]
    parts = _dot(jnp.concatenate(_split3(lf), axis=0), tri_ref[...])
    n = lf.shape[0]
    c = parts[:n] + parts[n:2 * n] + parts[2 * n:] + carry_scr[:, 0:1]
    c_ref[0] = c[:H_B] * c_scale
    carry_scr[...] = jnp.broadcast_to(c[:, tm - 1:tm], carry_scr.shape)


def _projections(x, g_kv, g_q, w_kv, w_ft, b_f, w_qg, *, tm, q_scale, c_scale):
    b, t, d = x.shape
    tri = (jnp.arange(tm)[:, None] <= jnp.arange(tm)[None, :]).astype(BF16)
    tok = lambda i, j: (i, j, 0)
    hm = lambda i, j: (i, 0, j)
    tok_spec = pl.BlockSpec((1, tm, d), tok)
    hm_spec = pl.BlockSpec((1, H_B, tm), hm)
    return pl.pallas_call(
        functools.partial(_proj_kernel, q_scale=q_scale, c_scale=c_scale),
        name="kv_qg_projections",
        grid=(b, t // tm),
        in_specs=[tok_spec, _const_spec(g_kv.shape), _const_spec(g_q.shape), _const_spec(w_kv.shape),
                  _const_spec(w_ft.shape), _const_spec(b_f.shape), _const_spec(tri.shape),
                  _const_spec(w_qg.shape)],
        out_specs=[tok_spec, tok_spec, tok_spec, tok_spec, hm_spec, hm_spec, tok_spec, tok_spec],
        out_shape=[jax.ShapeDtypeStruct((b, t, d), F32), jax.ShapeDtypeStruct((b, t, d), F32),
                   jax.ShapeDtypeStruct((b, t, d), BF16), jax.ShapeDtypeStruct((b, t, d), BF16),
                   jax.ShapeDtypeStruct((b, H_B, t), F32), jax.ShapeDtypeStruct((b, H_B, t), F32),
                   jax.ShapeDtypeStruct((b, t, d), BF16), jax.ShapeDtypeStruct((b, t, d), F32)],
        scratch_shapes=[pltpu.VMEM((ROWS_BF16, LANES), F32)],
        compiler_params=_params("parallel", "arbitrary"),
    )(x, g_kv, g_q, w_kv, w_ft, b_f, tri, w_qg)


def _attn_kernel(q_ref, k_ref, v_ref, c_ref, gate_ref, o_ref, *scr, dh):
    qi = pl.program_id(2)
    tq = q_ref.shape[1]
    lanes = [slice(h * dh, (h + 1) * dh) for h in range(q_ref.shape[2] // dh)]
    stats = [scr[3 * h:3 * h + 3] for h in range(len(lanes))]
    for m_scr, l_scr, acc_scr in stats:
        m_scr[...] = jnp.full_like(m_scr, -jnp.inf)
        l_scr[...] = jnp.zeros_like(l_scr)
        acc_scr[...] = jnp.zeros_like(acc_scr)

    def block(kj, diagonal):
        start = pl.multiple_of(kj * tq, tq)
        if diagonal:
            row = lax.broadcasted_iota(jnp.int32, (tq, tq), 0)
            col = lax.broadcasted_iota(jnp.int32, (tq, tq), 1)
            visible = col <= row
        for h, ln in enumerate(lanes):
            m_scr, l_scr, acc_scr = stats[h]
            ks = k_ref[0, pl.ds(start, tq), ln]
            vs = v_ref[0, pl.ds(start, tq), ln]
            s = _dot_nt(q_ref[0, :, ln], ks) - c_ref[0, h, pl.ds(kj, 1), :]
            if diagonal:
                s = jnp.where(visible, s, -jnp.inf)
            chunks = [s[:, j * dh:(j + 1) * dh] for j in range(tq // dh)]
            m_old = m_scr[...]
            m_new = jnp.maximum(m_old, jnp.max(functools.reduce(jnp.maximum, chunks),
                                               axis=-1, keepdims=True))
            parts = [jnp.exp2(c - m_new) for c in chunks]
            alpha = jnp.exp2(m_old - m_new)
            m_scr[...] = m_new
            l_scr[...] = alpha * l_scr[...] + jnp.sum(functools.reduce(jnp.add, parts),
                                                      axis=-1, keepdims=True)
            p = jnp.concatenate([part.astype(BF16) for part in parts], axis=1)
            acc_scr[...] = alpha * acc_scr[...] + _dot(p, vs)

    def body(kj, _):
        block(kj, False)
        return 0

    lax.fori_loop(0, qi, body, 0)
    block(qi, True)
    for h, ln in enumerate(lanes):
        _, l_scr, acc_scr = stats[h]
        o_ref[0, :, ln] = (_sigmoid(gate_ref[0, :, ln]) * (acc_scr[...] / l_scr[...])).astype(o_ref.dtype)


def _prompt_attention(q, kb, vb, c, gate, *, tq, heads):
    b, s, d = q.shape
    dh = d // H_B
    nq = s // tq
    c4 = c.reshape(b, H_B, nq, tq)
    qspec = pl.BlockSpec((1, tq, heads * dh), lambda i, h, j: (i, j, h))
    kvspec = pl.BlockSpec((1, s, heads * dh), lambda i, h, j: (i, 0, h))
    return pl.pallas_call(
        functools.partial(_attn_kernel, dh=dh),
        name="prompt_attention",
        grid=(b, H_B // heads, nq),
        in_specs=[qspec, kvspec, kvspec,
                  pl.BlockSpec((1, heads, nq, tq), lambda i, h, j: (i, h, 0, 0)), qspec],
        out_specs=qspec,
        out_shape=jax.ShapeDtypeStruct((b, s, d), BF16),
        scratch_shapes=[pltpu.VMEM((tq, dh), F32)] * (3 * heads),
        compiler_params=_params("parallel", "parallel", "arbitrary"),
    )(q, kb, vb, c4, gate)


def _sample_attn_kernel(pt_ref, q_ref, gate_ref, knew_ref, vnew_ref, lfnew_ref, cum_ref, cumnew_ref,
                        k_hbm, v_hbm, lf_hbm, o_ref, m_scr, l_scr, acc_scr, carry_scr,
                        k_buf, v_buf, lf_buf, k_sem, v_sem, lf_sem, *, group, t_new):
    b = pl.program_id(0)
    p = pl.program_id(1)
    n_steps = pl.num_programs(1)
    nrow = q_ref.shape[0]
    ncol = k_hbm.shape[1]
    stride = H_B
    q = q_ref[...]

    step = b * n_steps + p
    total_steps = pl.num_programs(0) * n_steps

    def page_copies(step_idx, slot):
        bb = step_idx // n_steps
        first = (step_idx % n_steps) * group
        copies = []
        for gi in range(group):
            page = pt_ref[bb, first + gi]
            dst = slot * group + gi
            copies += [pltpu.make_async_copy(k_hbm.at[page], k_buf.at[dst], k_sem.at[slot]),
                       pltpu.make_async_copy(v_hbm.at[page], v_buf.at[dst], v_sem.at[slot]),
                       pltpu.make_async_copy(lf_hbm.at[page], lf_buf.at[dst], lf_sem.at[slot])]
        return copies

    @pl.when(step == 0)
    def _():
        for ahead in range(N_SLOTS - 1):
            for cp in page_copies(ahead, ahead):
                cp.start()

    @pl.when(step + (N_SLOTS - 1) < total_steps)
    def _():
        nxt = step + (N_SLOTS - 1)
        for cp in page_copies(nxt, nxt % N_SLOTS):
            cp.start()

    slot = step % N_SLOTS
    for cp in page_copies(step, slot):
        cp.wait()
    k_refs = [k_buf.at[slot * group + gi] for gi in range(group)]
    v_refs = [v_buf.at[slot * group + gi] for gi in range(group)]
    lf_refs = [lf_buf.at[slot * group + gi] for gi in range(group)]

    @pl.when(p == 0)
    def _():
        m_scr[...] = jnp.full_like(m_scr, -jnp.inf)
        l_scr[...] = jnp.zeros_like(l_scr)
        acc_scr[...] = jnp.zeros_like(acc_scr)
        carry_scr[...] = jnp.zeros_like(carry_scr)

    def same_head(n):
        row = lax.broadcasted_iota(jnp.int32, (nrow, n), 0)
        col = lax.broadcasted_iota(jnp.int32, (nrow, n), 1)
        return row % stride == col % stride, row, col

    def update(scores, values):
        m_old = m_scr[:, 0:1]
        m_new = m_old
        for s in scores:
            m_new = jnp.maximum(m_new, jnp.max(s, axis=-1, keepdims=True))
        alpha = jnp.exp2(m_old - m_new)
        l_new = alpha * l_scr[:, 0:1]
        acc = alpha * acc_scr[...]
        for s, v in zip(scores, values):
            pr = jnp.exp2(s - m_new)
            l_new = l_new + jnp.sum(pr, axis=-1, keepdims=True)
            acc = acc + _dot(pr.astype(BF16), v)
        l_scr[...] = jnp.broadcast_to(l_new, l_scr.shape)
        acc_scr[...] = acc
        m_scr[...] = jnp.broadcast_to(m_new, m_scr.shape)

    def running_sum(lf, cum_ref):
        n = lf.shape[0]
        parts = _dot(jnp.concatenate(_split3_f32(lf), axis=0).astype(BF16), cum_ref[...])
        return parts[:n] + parts[n:2 * n] + parts[2 * n:3 * n]

    lf = jnp.concatenate([r[...] for r in lf_refs], axis=0) * LOG2E
    within = running_sum(lf, cum_ref)
    mask, _, _ = same_head(ncol)
    carry = carry_scr[:, 0:1]
    scores, values = [], []
    for gi in range(group):
        w = within[gi * H_B:(gi + 1) * H_B]
        kb = k_refs[gi][...].astype(BF16)
        s = _dot_nt(q, kb) - jnp.tile(carry + w, (t_new, 1))
        scores.append(jnp.where(mask, s, -jnp.inf))
        values.append(v_refs[gi][...].astype(BF16))
        carry = carry + w[:, ncol - 1:ncol]
    carry_scr[...] = jnp.broadcast_to(carry, carry_scr.shape)
    update(scores, values)

    @pl.when(p == pl.num_programs(1) - 1)
    def _():
        n = knew_ref.shape[0]
        x = lfnew_ref[...] * LOG2E
        zeros = jnp.zeros_like(x)
        bias_new = carry + running_sum(jnp.concatenate([x, zeros], axis=0), cumnew_ref)[:H_B]
        mask_new, row, col = same_head(n)
        tok = col // stride
        visible = mask_new & (tok // t_new == b) & (tok % t_new <= row // stride)
        s_new = _dot_nt(q, knew_ref[...]) - jnp.tile(bias_new, (t_new, 1))
        update([jnp.where(visible, s_new, -jnp.inf)], [vnew_ref[...]])
        o_ref[...] = _sigmoid(gate_ref[...]) * (acc_scr[...] / l_scr[:, 0:1])


def _sample_attention(page_table, q, gate, k_new, v_new, lf_new, cache_k, cache_v, cache_lf, *, group, t_new):
    n_rows, dh = q.shape
    nrow = t_new * H_B
    bd = n_rows // nrow
    n_pages = page_table.shape[1]
    ncol = cache_k.shape[1]
    page = cache_lf.shape[2]
    n_tok = lf_new.shape[1]
    key_of = jnp.arange(ncol)[None, :] // H_B
    cum = (jnp.arange(page)[:, None] <= key_of).astype(BF16)
    tok, tok_of = jnp.arange(n_tok)[:, None], jnp.arange(n_tok * H_B)[None, :] // H_B
    cum_new = ((tok <= tok_of) & (tok // t_new == tok_of // t_new)).astype(BF16)
    per_b = pl.BlockSpec((nrow, dh), lambda i, j, pt: (i, 0))
    whole = lambda a: pl.BlockSpec(a.shape, lambda i, j, pt: (0,) * a.ndim)

    hbm = pl.BlockSpec(memory_space=pl.ANY)
    in_specs = [per_b, per_b, whole(k_new), whole(v_new), whole(lf_new), whole(cum), whole(cum_new),
                hbm, hbm, hbm]
    n_buf = N_SLOTS * group
    grid_spec = pltpu.PrefetchScalarGridSpec(
        num_scalar_prefetch=1,
        grid=(bd, n_pages // group),
        in_specs=in_specs,
        out_specs=per_b,
        scratch_shapes=[pltpu.VMEM((nrow, LANES), F32), pltpu.VMEM((nrow, LANES), F32),
                        pltpu.VMEM((nrow, dh), F32), pltpu.VMEM((H_B, LANES), F32),
                        pltpu.VMEM((n_buf, ncol, dh), cache_k.dtype),
                        pltpu.VMEM((n_buf, ncol, dh), cache_v.dtype),
                        pltpu.VMEM((n_buf, H_B, page), cache_lf.dtype),
                        pltpu.SemaphoreType.DMA((N_SLOTS,)), pltpu.SemaphoreType.DMA((N_SLOTS,)),
                        pltpu.SemaphoreType.DMA((N_SLOTS,))],
    )
    return pl.pallas_call(
        functools.partial(_sample_attn_kernel, group=group, t_new=t_new),
        name="sample_attention",
        grid_spec=grid_spec,
        out_shape=jax.ShapeDtypeStruct((n_rows, dh), F32),
        compiler_params=_params("arbitrary", "arbitrary"),
    )(page_table, q, gate, k_new, v_new, lf_new, cum, cum_new, cache_k, cache_v, cache_lf)


def _pick_tile(n, target):
    t = min(n, target)
    while n % t:
        t //= 2
    return t


def kernel(x_prompt, x_sample, state_ret, cache_k, cache_v, cache_logf, page_table,
           norm_mix, w_ret_in, gn_ret, w_ret_out, norm_kv, w_kvf, b_f,
           w_b_qg, w_b_out, norm_ffn, w_ffn_up, w_ffn_down, norm_final):
    b, t, d = x_prompt.shape
    bd, td, _ = x_sample.shape
    dk = d // H_A
    dh = d // H_B
    n_pool, page = cache_k.shape[0], cache_k.shape[1]
    past = page_table.shape[1] * page

    perm = jnp.concatenate([jnp.arange(0, dk, 2), jnp.arange(1, dk, 2)])
    head_cols = (jnp.arange(H_A)[:, None] * dk + perm[None, :]).reshape(-1)
    cols = jnp.concatenate([head_cols, d + head_cols, 2 * d + jnp.arange(2 * d)])
    w_in = w_ret_in[0][:, cols].astype(BF16)
    perm_in = (perm[:, None] == jnp.arange(dk)[None, :]).astype(BF16)
    perms = (perm_in, perm_in.T)
    w_out = w_ret_out[0].astype(BF16)
    w_up = w_ffn_up.astype(BF16)
    w_down = w_ffn_down.astype(BF16)
    w_kv = w_kvf[:, :2 * d].astype(BF16)
    w_ft = jnp.zeros((ROWS_BF16, d), BF16).at[:H_B].set(w_kvf[:, 2 * d:].T.astype(BF16))
    b_col = jnp.zeros((ROWS_BF16, 1), F32).at[:H_B, 0].set(b_f.astype(F32))
    w_qg = w_b_qg[0].astype(BF16)
    w_o = w_b_out[0].astype(BF16)
    row = lambda v: v.reshape(1, d).astype(F32)
    g_mix0, g_mix1 = row(norm_mix[0]), row(norm_mix[1])
    g_ffn0, g_ffn1 = row(norm_ffn[0]), row(norm_ffn[1])
    g_kv, g_fin, gn = row(norm_kv), row(norm_final), row(gn_ret[0])
    q_scale = LOG2E * dh ** -0.5

    def trunk_tokens(h1, m_tile):
        bb, tt, _ = h1.shape
        h2 = _ffn_block(h1.reshape(bb * tt, d), g_ffn0, w_up, w_down, 0, tm=m_tile).reshape(bb, tt, d)
        return h2, _projections(h2, g_kv, g_mix1, w_kv, w_ft, b_col, w_qg, tm=m_tile,
                                q_scale=q_scale, c_scale=LOG2E)

    tc = _pick_tile(t, 512)
    chunk = _pick_tile(tc, 256)
    cos_p, sin_p = _rope_tables(jnp.arange(t), dk // 2)
    h1, s_prompt = _retention_layer(
        x_prompt, None, cos_p, sin_p, g_mix0, w_in, gn, w_out,
        _decay_tables(chunk, chunk, dk), perms, tc=tc, chunk=chunk)
    h2, (k_p, v_p, kb_p, vb_p, lf_p, c_p, q_p, gate_p) = trunk_tokens(h1, tc)
    a_p = _prompt_attention(q_p, kb_p, vb_p, c_p, gate_p, tq=_pick_tile(t, 512), heads=8)
    y_prompt = _ffn_block(h2.reshape(b * t, d), g_ffn1, w_up, w_down, 1, tm=tc,
                          attn=(a_p.reshape(b * t, d), w_o), g_final=g_fin).reshape(b, t, d)

    rows = LANES
    cos_s, sin_s = _rope_tables(past + jnp.arange(rows), dk // 2)
    h1s, s_sample = _retention_layer(
        x_sample, state_ret[0], cos_s, sin_s, g_mix0, w_in, gn, w_out,
        _decay_tables(rows, td, dk), perms, tc=rows, chunk=rows, nseq=_pick_tile(bd, 4))
    h1s = h1s.reshape(1, bd * td, d)
    h2s, (k_s, v_s, kb_s, vb_s, lf_s, _, q_s, gate_s) = trunk_tokens(h1s, bd * td)
    per_head = lambda a: a.reshape(bd * td * H_B, dh)
    lf_s_tok = jnp.swapaxes(lf_s[0], 0, 1)
    a_s = _sample_attention(
        page_table, per_head(q_s), per_head(gate_s), per_head(kb_s), per_head(vb_s),
        lf_s[0],
        cache_k.reshape(n_pool, page * H_B, dh), cache_v.reshape(n_pool, page * H_B, dh),
        jnp.swapaxes(cache_logf, 1, 2),
        group=8, t_new=td)
    y_sample = _ffn_block(h2s.reshape(bd * td, d), g_ffn1, w_up, w_down, 1, tm=bd * td,
                          attn=(a_s.reshape(bd * td, d), w_o), g_final=g_fin).reshape(bd, td, d)

    return (y_prompt, y_sample, s_prompt[None],
            k_p.reshape(b, t, H_B, dh), v_p.reshape(b, t, H_B, dh), jnp.swapaxes(lf_p, 1, 2),
            s_sample[None],
            k_s.reshape(bd, td, H_B, dh), v_s.reshape(bd, td, H_B, dh),
            lf_s_tok.reshape(bd, td, H_B))
```

```python
import functools
import math

import jax
import jax.numpy as jnp
from jax import lax
from jax.experimental import pallas as pl
from jax.experimental.pallas import tpu as pltpu

H_A = 4
H_B = 8
RMS_EPS = 1e-6
GN_EPS = 1e-5
ROPE_BASE = 10000.0
LOG2E = math.log2(math.e)
LANES = 128
ROWS_BF16 = 16
N_SLOTS = 3
VMEM_LIMIT_BYTES = 56 * 1024 * 1024

F32 = jnp.float32
BF16 = jnp.bfloat16


def _dot(a, b):
    return jnp.dot(a, b, preferred_element_type=F32)


def _dot_nt(a, b):
    return lax.dot_general(a, b, (((1,), (1,)), ((), ())), preferred_element_type=F32)


def _dot_tn(a, b):
    return lax.dot_general(a, b, (((0,), (0,)), ((), ())), preferred_element_type=F32)


def _rmsnorm(x, g):
    return x * lax.rsqrt(jnp.mean(x * x, axis=-1, keepdims=True) + RMS_EPS) * g


def _sigmoid(x):
    return 1.0 / (1.0 + jnp.exp(-x))


def _log_sigmoid(x):
    return jnp.minimum(x, 0.0) - jnp.log1p(jnp.exp(-jnp.abs(x)))


def _split3(x):
    return tuple(part.astype(BF16) for part in _split3_f32(x))


def _split3_f32(x):
    hi = x.astype(BF16).astype(F32)
    r1 = x - hi
    mid = r1.astype(BF16).astype(F32)
    lo = (r1 - mid).astype(BF16).astype(F32)
    return hi, mid, lo


def _const_spec(shape):
    nd = len(shape)
    return pl.BlockSpec(shape, lambda *_: (0,) * nd, pipeline_mode=pl.Buffered(1))


def _params(*sem):
    return pltpu.CompilerParams(dimension_semantics=sem, vmem_limit_bytes=VMEM_LIMIT_BYTES)


def _permute_rows(p_ref, s):
    p = p_ref[...]
    hi, mid, lo = _split3(s)
    return _dot(p, hi) + _dot(p, mid) + _dot(p, lo)


def _permute_cols(z, p_ref):
    p = p_ref[...]
    hi, mid, lo = _split3(z)
    return _dot(hi, p) + _dot(mid, p) + _dot(lo, p)


def _ret_kernel(*refs, chunk, has_state):
    refs = list(refs)
    x_ref = refs.pop(0)
    s0_ref = refs.pop(0) if has_state else None
    (cos_ref, sin_ref, g_ref, win_ref, gn_ref, wout_ref, intra_ref, qdec_ref, kdec_ref, cdec_ref,
     pcol_ref, prow_ref, h_ref, sfin_ref, s_scr, gate_scr, *pad_scr) = refs
    t = pl.program_id(1)
    nseq, tx, d = x_ref.shape
    tc = gate_scr.shape[0] // nseq
    dk = d // H_A
    half = dk // 2

    @pl.when(t == 0)
    def _():
        for i in range(nseq):
            for h in range(H_A):
                s_scr[i * H_A + h] = s0_ref[i, h] if has_state else jnp.zeros((dk, dk), F32)

    if tx == tc:
        x = x_ref[...].reshape(nseq * tc, d)
    else:
        xpad_scr, = pad_scr
        xpad_scr[...] = jnp.zeros_like(xpad_scr)
        for i in range(nseq):
            xpad_scr[i * tc:i * tc + tx, :] = x_ref[i]
        x = xpad_scr[...]
    hn = _rmsnorm(x, g_ref[...]).astype(BF16)
    cos = cos_ref[...]
    sin = sin_ref[...]

    def rot(z):
        z1, z2 = z[:, :half], z[:, half:]
        return jnp.concatenate([z1 * cos - z2 * sin, z1 * sin + z2 * cos], axis=1)

    for h in range(H_A):
        def proj(j, h=h):
            return _dot(hn, win_ref[:, j * d + h * dk:j * d + (h + 1) * dk])

        q = rot(proj(0)).astype(BF16)
        k = rot(proj(1)) * (dk ** -0.5)
        if has_state:
            q = _dot(q, pcol_ref[...]).astype(BF16)
            k = _permute_cols(k, pcol_ref)
        v = proj(2).astype(BF16)
        g = proj(3)
        for c in range(nseq * tc // chunk):
            r = slice(c * chunk, (c + 1) * chunk)
            st = (c * chunk // tc) * H_A + h
            qc, kc, vc = q[r], k[r], v[r]
            s_old = s_scr[st]
            scores = _dot_nt(qc, kc.astype(BF16)) * intra_ref[h]
            o = _dot(scores.astype(BF16), vc) + _dot(qc, s_old.astype(BF16)) * qdec_ref[h]
            s_scr[st] = s_old * cdec_ref[h] + _dot_tn((kc * kdec_ref[h]).astype(BF16), vc)
            mu = jnp.mean(o, axis=-1, keepdims=True)
            oc = o - mu
            var = jnp.mean(oc * oc, axis=-1, keepdims=True)
            on = oc * lax.rsqrt(var + GN_EPS) * gn_ref[:, h * dk:(h + 1) * dk]
            gc = g[r]
            gate_scr[r, h * dk:(h + 1) * dk] = (gc * _sigmoid(gc) * on).astype(BF16)

    out = x + _dot(gate_scr[...], wout_ref[...])
    if tx == tc:
        h_ref[...] = out.reshape(nseq, tc, d)
    else:
        for i in range(nseq):
            h_ref[i] = out[i * tc:i * tc + tx]

    @pl.when(t == pl.num_programs(1) - 1)
    def _():
        for i in range(nseq):
            for h in range(H_A):
                s_fin = s_scr[i * H_A + h]
                sfin_ref[i, h] = s_fin if has_state else _permute_rows(prow_ref, s_fin)


def _retention_layer(x, s0, cos, sin, g, w_in, gn, w_out, dec, perms, *, tc, chunk, nseq=1):
    b, t, d = x.shape
    dk = d // H_A
    tx = min(t, tc)
    state_spec = pl.BlockSpec((nseq, H_A, dk, dk), lambda i, j: (i, 0, 0, 0))
    tok_spec = pl.BlockSpec((nseq, tx, d), lambda i, j: (i, j, 0))
    pad_scratch = [] if tx == tc else [pltpu.VMEM((nseq * tc, d), F32)]
    cos, sin = jnp.tile(cos, (nseq, 1)), jnp.tile(sin, (nseq, 1))
    rope_spec = pl.BlockSpec((nseq * tc, dk // 2), lambda i, j: (j, 0))
    consts = [g, w_in, gn, w_out, *dec, *perms]
    args = [x] + ([] if s0 is None else [s0]) + [cos, sin] + consts
    specs = ([tok_spec] + ([] if s0 is None else [state_spec]) + [rope_spec, rope_spec]
             + [_const_spec(a.shape) for a in consts])
    return pl.pallas_call(
        functools.partial(_ret_kernel, chunk=chunk, has_state=s0 is not None),
        name="retention_layer",
        grid=(b // nseq, t // tx),
        in_specs=specs,
        out_specs=[tok_spec, state_spec],
        out_shape=[jax.ShapeDtypeStruct((b, t, d), F32),
                   jax.ShapeDtypeStruct((b, H_A, dk, dk), F32)],
        scratch_shapes=[pltpu.VMEM((nseq * H_A, dk, dk), F32), pltpu.VMEM((nseq * tc, d), BF16),
                        *pad_scratch],
        compiler_params=_params("parallel", "arbitrary"),
    )(*args)


def _decay_tables(rows, c_true, dk):
    lg = jnp.log(1.0 - jnp.exp2(-5.0 - jnp.arange(H_A, dtype=F32)))
    i = jnp.arange(rows, dtype=F32)
    diff = i[:, None] - i[None, :]
    intra = jnp.where(diff >= 0, jnp.exp(lg[:, None, None] * jnp.maximum(diff, 0.0)), 0.0)
    qdec = jnp.exp(lg[:, None] * (i[None, :] + 1.0))
    kdec = jnp.exp(lg[:, None] * (c_true - 1.0 - i[None, :]))
    cdec = jnp.exp(lg * c_true)
    return (intra,
            jnp.broadcast_to(qdec[:, :, None], (H_A, rows, dk)),
            jnp.broadcast_to(kdec[:, :, None], (H_A, rows, dk)),
            jnp.broadcast_to(cdec[:, None, None], (H_A, 1, dk)))


def _rope_tables(pos, half):
    inv = 1.0 / (ROPE_BASE ** jnp.linspace(0.0, 1.0, half, dtype=F32))
    ang = pos.astype(F32)[:, None] * inv[None, :]
    return jnp.cos(ang), jnp.sin(ang)


def _ffn_kernel(*refs, ff, has_attn, final_norm):
    refs = list(refs)
    x_ref = refs.pop(0)
    if has_attn:
        a_ref, wo_ref = refs.pop(0), refs.pop(0)
    g_ref, wup_ref, wdown_ref = refs.pop(0), refs.pop(0), refs.pop(0)
    if final_norm:
        gfin_ref = refs.pop(0)
    o_ref, = refs

    x = x_ref[...]
    if has_attn:
        x = x + _dot(a_ref[...].astype(BF16), wo_ref[...])
    hn = _rmsnorm(x, g_ref[...]).astype(BF16)
    gate = _dot(hn, wup_ref[:, :ff])
    up = _dot(hn, wup_ref[:, ff:])
    act = (gate * _sigmoid(gate) * up).astype(BF16)
    y = x + _dot(act, wdown_ref[...])
    if final_norm:
        y = _rmsnorm(y, gfin_ref[...])
    o_ref[...] = y


def _layer_spec(w, layer):
    return pl.BlockSpec((None,) + w.shape[1:], lambda *_: (layer, 0, 0), pipeline_mode=pl.Buffered(1))


def _ffn_block(x, g, w_up, w_down, layer, *, tm, attn=None, g_final=None):
    m, d = x.shape
    ff = w_down.shape[1]
    row = lambda i: (i, 0)
    args, specs = [x], [pl.BlockSpec((tm, d), row)]
    if attn is not None:
        a, w_o = attn
        args += [a, w_o]
        specs += [pl.BlockSpec((tm, d), row), _const_spec(w_o.shape)]
    args += [g, w_up, w_down]
    specs += [_const_spec(g.shape), _layer_spec(w_up, layer), _layer_spec(w_down, layer)]
    if g_final is not None:
        args.append(g_final)
        specs.append(_const_spec(g_final.shape))
    return pl.pallas_call(
        functools.partial(_ffn_kernel, ff=ff, has_attn=attn is not None,
                          final_norm=g_final is not None),
        name="ffn_block",
        grid=(m // tm,),
        in_specs=specs,
        out_specs=pl.BlockSpec((tm, d), row),
        out_shape=jax.ShapeDtypeStruct((m, d), F32),
        compiler_params=_params("parallel"),
    )(*args)


def _proj_kernel(x_ref, gkv_ref, gq_ref, wkv_ref, wft_ref, bf_ref, tri_ref, wqg_ref,
                 k_ref, v_ref, kb_ref, vb_ref, lf_ref, c_ref, q_ref, gate_ref, carry_scr,
                 *, q_scale, c_scale):
    t = pl.program_id(1)
    d = x_ref.shape[2]
    tm = x_ref.shape[1]

    @pl.when(t == 0)
    def _():
        carry_scr[...] = jnp.zeros_like(carry_scr)

    x = x_ref[0]
    xn = x * lax.rsqrt(jnp.mean(x * x, axis=-1, keepdims=True) + RMS_EPS)
    hkv = (xn * gkv_ref[...]).astype(BF16)
    hq = (xn * gq_ref[...]).astype(BF16)

    kv = _dot(hkv, wkv_ref[...])
    k, v = kv[:, :d], kv[:, d:]
    k_ref[0] = k
    v_ref[0] = v
    kb_ref[0] = k.astype(BF16)
    vb_ref[0] = v.astype(BF16)

    qg = _dot(hq, wqg_ref[...])
    q_ref[0] = (qg[:, :d] * q_scale).astype(BF16)
    gate_ref[0] = qg[:, d:]

    lf = _log_sigmoid(_dot_nt(wft_ref[...], hkv) + bf_ref[...])
    lf_ref[0] = lf[:H_B]
    parts = _dot(jnp.concatenate(_split3(lf), axis=0), tri_ref[...])
    n = lf.shape[0]
    c = parts[:n] + parts[n:2 * n] + parts[2 * n:] + carry_scr[:, 0:1]
    c_ref[0] = c[:H_B] * c_scale
    carry_scr[...] = jnp.broadcast_to(c[:, tm - 1:tm], carry_scr.shape)


def _projections(x, g_kv, g_q, w_kv, w_ft, b_f, w_qg, *, tm, q_scale, c_scale):
    b, t, d = x.shape
    tri = (jnp.arange(tm)[:, None] <= jnp.arange(tm)[None, :]).astype(BF16)
    tok = lambda i, j: (i, j, 0)
    hm = lambda i, j: (i, 0, j)
    tok_spec = pl.BlockSpec((1, tm, d), tok)
    hm_spec = pl.BlockSpec((1, H_B, tm), hm)
    return pl.pallas_call(
        functools.partial(_proj_kernel, q_scale=q_scale, c_scale=c_scale),
        name="kv_qg_projections",
        grid=(b, t // tm),
        in_specs=[tok_spec, _const_spec(g_kv.shape), _const_spec(g_q.shape), _const_spec(w_kv.shape),
                  _const_spec(w_ft.shape), _const_spec(b_f.shape), _const_spec(tri.shape),
                  _const_spec(w_qg.shape)],
        out_specs=[tok_spec, tok_spec, tok_spec, tok_spec, hm_spec, hm_spec, tok_spec, tok_spec],
        out_shape=[jax.ShapeDtypeStruct((b, t, d), F32), jax.ShapeDtypeStruct((b, t, d), F32),
                   jax.ShapeDtypeStruct((b, t, d), BF16), jax.ShapeDtypeStruct((b, t, d), BF16),
                   jax.ShapeDtypeStruct((b, H_B, t), F32), jax.ShapeDtypeStruct((b, H_B, t), F32),
                   jax.ShapeDtypeStruct((b, t, d), BF16), jax.ShapeDtypeStruct((b, t, d), F32)],
        scratch_shapes=[pltpu.VMEM((ROWS_BF16, LANES), F32)],
        compiler_params=_params("parallel", "arbitrary"),
    )(x, g_kv, g_q, w_kv, w_ft, b_f, tri, w_qg)


def _attn_kernel(q_ref, k_ref, v_ref, c_ref, gate_ref, o_ref, *scr, dh):
    qi = pl.program_id(2)
    tq = q_ref.shape[1]
    lanes = [slice(h * dh, (h + 1) * dh) for h in range(q_ref.shape[2] // dh)]
    stats = [scr[3 * h:3 * h + 3] for h in range(len(lanes))]
    for m_scr, l_scr, acc_scr in stats:
        m_scr[...] = jnp.full_like(m_scr, -jnp.inf)
        l_scr[...] = jnp.zeros_like(l_scr)
        acc_scr[...] = jnp.zeros_like(acc_scr)

    def block(kj, diagonal):
        start = pl.multiple_of(kj * tq, tq)
        if diagonal:
            row = lax.broadcasted_iota(jnp.int32, (tq, tq), 0)
            col = lax.broadcasted_iota(jnp.int32, (tq, tq), 1)
            visible = col <= row
        for h, ln in enumerate(lanes):
            m_scr, l_scr, acc_scr = stats[h]
            ks = k_ref[0, pl.ds(start, tq), ln]
            vs = v_ref[0, pl.ds(start, tq), ln]
            s = _dot_nt(q_ref[0, :, ln], ks) - c_ref[0, h, pl.ds(kj, 1), :]
            if diagonal:
                s = jnp.where(visible, s, -jnp.inf)
            chunks = [s[:, j * dh:(j + 1) * dh] for j in range(tq // dh)]
            m_old = m_scr[...]
            m_new = jnp.maximum(m_old, jnp.max(functools.reduce(jnp.maximum, chunks),
                                               axis=-1, keepdims=True))
            parts = [jnp.exp2(c - m_new) for c in chunks]
            alpha = jnp.exp2(m_old - m_new)
            m_scr[...] = m_new
            l_scr[...] = alpha * l_scr[...] + jnp.sum(functools.reduce(jnp.add, parts),
                                                      axis=-1, keepdims=True)
            p = jnp.concatenate([part.astype(BF16) for part in parts], axis=1)
            acc_scr[...] = alpha * acc_scr[...] + _dot(p, vs)

    def body(kj, _):
        block(kj, False)
        return 0

    lax.fori_loop(0, qi, body, 0)
    block(qi, True)
    for h, ln in enumerate(lanes):
        _, l_scr, acc_scr = stats[h]
        o_ref[0, :, ln] = (_sigmoid(gate_ref[0, :, ln]) * (acc_scr[...] / l_scr[...])).astype(o_ref.dtype)


def _prompt_attention(q, kb, vb, c, gate, *, tq, heads):
    b, s, d = q.shape
    dh = d // H_B
    nq = s // tq
    c4 = c.reshape(b, H_B, nq, tq)
    qspec = pl.BlockSpec((1, tq, heads * dh), lambda i, h, j: (i, j, h))
    kvspec = pl.BlockSpec((1, s, heads * dh), lambda i, h, j: (i, 0, h))
    return pl.pallas_call(
        functools.partial(_attn_kernel, dh=dh),
        name="prompt_attention",
        grid=(b, H_B // heads, nq),
        in_specs=[qspec, kvspec, kvspec,
                  pl.BlockSpec((1, heads, nq, tq), lambda i, h, j: (i, h, 0, 0)), qspec],
        out_specs=qspec,
        out_shape=jax.ShapeDtypeStruct((b, s, d), BF16),
        scratch_shapes=[pltpu.VMEM((tq, dh), F32)] * (3 * heads),
        compiler_params=_params("parallel", "parallel", "arbitrary"),
    )(q, kb, vb, c4, gate)


def _sample_attn_kernel(pt_ref, q_ref, gate_ref, knew_ref, vnew_ref, lfnew_ref, cum_ref, cumnew_ref,
                        k_hbm, v_hbm, lf_hbm, o_ref, m_scr, l_scr, acc_scr, carry_scr,
                        k_buf, v_buf, lf_buf, k_sem, v_sem, lf_sem, *, group, t_new):
    b = pl.program_id(0)
    p = pl.program_id(1)
    n_steps = pl.num_programs(1)
    nrow = q_ref.shape[0]
    ncol = k_hbm.shape[1]
    stride = H_B
    q = q_ref[...]

    step = b * n_steps + p
    total_steps = pl.num_programs(0) * n_steps

    def page_copies(step_idx, slot):
        bb = step_idx // n_steps
        first = (step_idx % n_steps) * group
        copies = []
        for gi in range(group):
            page = pt_ref[bb, first + gi]
            dst = slot * group + gi
            copies += [pltpu.make_async_copy(k_hbm.at[page], k_buf.at[dst], k_sem.at[slot]),
                       pltpu.make_async_copy(v_hbm.at[page], v_buf.at[dst], v_sem.at[slot]),
                       pltpu.make_async_copy(lf_hbm.at[page], lf_buf.at[dst], lf_sem.at[slot])]
        return copies

    @pl.when(step == 0)
    def _():
        for ahead in range(N_SLOTS - 1):
            for cp in page_copies(ahead, ahead):
                cp.start()

    @pl.when(step + (N_SLOTS - 1) < total_steps)
    def _():
        nxt = step + (N_SLOTS - 1)
        for cp in page_copies(nxt, nxt % N_SLOTS):
            cp.start()

    slot = step % N_SLOTS
    for cp in page_copies(step, slot):
        cp.wait()
    k_refs = [k_buf.at[slot * group + gi] for gi in range(group)]
    v_refs = [v_buf.at[slot * group + gi] for gi in range(group)]
    lf_refs = [lf_buf.at[slot * group + gi] for gi in range(group)]

    @pl.when(p == 0)
    def _():
        m_scr[...] = jnp.full_like(m_scr, -jnp.inf)
        l_scr[...] = jnp.zeros_like(l_scr)
        acc_scr[...] = jnp.zeros_like(acc_scr)
        carry_scr[...] = jnp.zeros_like(carry_scr)

    def same_head(n):
        row = lax.broadcasted_iota(jnp.int32, (nrow, n), 0)
        col = lax.broadcasted_iota(jnp.int32, (nrow, n), 1)
        return row % stride == col % stride, row, col

    def update(scores, values):
        m_old = m_scr[:, 0:1]
        m_new = m_old
        for s in scores:
            m_new = jnp.maximum(m_new, jnp.max(s, axis=-1, keepdims=True))
        alpha = jnp.exp2(m_old - m_new)
        l_new = alpha * l_scr[:, 0:1]
        acc = alpha * acc_scr[...]
        for s, v in zip(scores, values):
            pr = jnp.exp2(s - m_new)
            l_new = l_new + jnp.sum(pr, axis=-1, keepdims=True)
            acc = acc + _dot(pr.astype(BF16), v)
        l_scr[...] = jnp.broadcast_to(l_new, l_scr.shape)
        acc_scr[...] = acc
        m_scr[...] = jnp.broadcast_to(m_new, m_scr.shape)

    def running_sum(lf, cum_ref):
        n = lf.shape[0]
        parts = _dot(jnp.concatenate(_split3_f32(lf), axis=0).astype(BF16), cum_ref[...])
        return parts[:n] + parts[n:2 * n] + parts[2 * n:3 * n]

    lf = jnp.concatenate([r[...] for r in lf_refs], axis=0) * LOG2E
    within = running_sum(lf, cum_ref)
    mask, _, _ = same_head(ncol)
    carry = carry_scr[:, 0:1]
    scores, values = [], []
    for gi in range(group):
        w = within[gi * H_B:(gi + 1) * H_B]
        kb = k_refs[gi][...].astype(BF16)
        s = _dot_nt(q, kb) - jnp.tile(carry + w, (t_new, 1))
        scores.append(jnp.where(mask, s, -jnp.inf))
        values.append(v_refs[gi][...].astype(BF16))
        carry = carry + w[:, ncol - 1:ncol]
    carry_scr[...] = jnp.broadcast_to(carry, carry_scr.shape)
    update(scores, values)

    @pl.when(p == pl.num_programs(1) - 1)
    def _():
        n = knew_ref.shape[0]
        x = lfnew_ref[...] * LOG2E
        zeros = jnp.zeros_like(x)
        bias_new = carry + running_sum(jnp.concatenate([x, zeros], axis=0), cumnew_ref)[:H_B]
        mask_new, row, col = same_head(n)
        tok = col // stride
        visible = mask_new & (tok // t_new == b) & (tok % t_new <= row // stride)
        s_new = _dot_nt(q, knew_ref[...]) - jnp.tile(bias_new, (t_new, 1))
        update([jnp.where(visible, s_new, -jnp.inf)], [vnew_ref[...]])
        o_ref[...] = _sigmoid(gate_ref[...]) * (acc_scr[...] / l_scr[:, 0:1])


def _sample_attention(page_table, q, gate, k_new, v_new, lf_new, cache_k, cache_v, cache_lf, *, group, t_new):
    n_rows, dh = q.shape
    nrow = t_new * H_B
    bd = n_rows // nrow
    n_pages = page_table.shape[1]
    ncol = cache_k.shape[1]
    page = cache_lf.shape[2]
    n_tok = lf_new.shape[1]
    key_of = jnp.arange(ncol)[None, :] // H_B
    cum = (jnp.arange(page)[:, None] <= key_of).astype(BF16)
    tok, tok_of = jnp.arange(n_tok)[:, None], jnp.arange(n_tok * H_B)[None, :] // H_B
    cum_new = ((tok <= tok_of) & (tok // t_new == tok_of // t_new)).astype(BF16)
    per_b = pl.BlockSpec((nrow, dh), lambda i, j, pt: (i, 0))
    whole = lambda a: pl.BlockSpec(a.shape, lambda i, j, pt: (0,) * a.ndim)

    hbm = pl.BlockSpec(memory_space=pl.ANY)
    in_specs = [per_b, per_b, whole(k_new), whole(v_new), whole(lf_new), whole(cum), whole(cum_new),
                hbm, hbm, hbm]
    n_buf = N_SLOTS * group
    grid_spec = pltpu.PrefetchScalarGridSpec(
        num_scalar_prefetch=1,
        grid=(bd, n_pages // group),
        in_specs=in_specs,
        out_specs=per_b,
        scratch_shapes=[pltpu.VMEM((nrow, LANES), F32), pltpu.VMEM((nrow, LANES), F32),
                        pltpu.VMEM((nrow, dh), F32), pltpu.VMEM((H_B, LANES), F32),
                        pltpu.VMEM((n_buf, ncol, dh), cache_k.dtype),
                        pltpu.VMEM((n_buf, ncol, dh), cache_v.dtype),
                        pltpu.VMEM((n_buf, H_B, page), cache_lf.dtype),
                        pltpu.SemaphoreType.DMA((N_SLOTS,)), pltpu.SemaphoreType.DMA((N_SLOTS,)),
                        pltpu.SemaphoreType.DMA((N_SLOTS,))],
    )
    return pl.pallas_call(
        functools.partial(_sample_attn_kernel, group=group, t_new=t_new),
        name="sample_attention",
        grid_spec=grid_spec,
        out_shape=jax.ShapeDtypeStruct((n_rows, dh), F32),
        compiler_params=_params("arbitrary", "arbitrary"),
    )(page_table, q, gate, k_new, v_new, lf_new, cum, cum_new, cache_k, cache_v, cache_lf)


TOKEN_TILE = 512
RET_TILE = 1024
RET_CHUNK = 256
ATTN_TILE = 512
ATTN_HEADS = H_B
SAMPLE_CHUNK = LANES
SAMPLE_SEQS = 4
PAGE_GROUP = 8


def _pick_tile(n, target):
    t = min(n, target)
    while n % t:
        t //= 2
    return t


def kernel(x_prompt, x_sample, state_ret, cache_k, cache_v, cache_logf, page_table,
           norm_mix, w_ret_in, gn_ret, w_ret_out, norm_kv, w_kvf, b_f,
           w_b_qg, w_b_out, norm_ffn, w_ffn_up, w_ffn_down, norm_final):
    b, t, d = x_prompt.shape
    bd, td, _ = x_sample.shape
    dk = d // H_A
    dh = d // H_B
    n_pool, page = cache_k.shape[0], cache_k.shape[1]
    past = page_table.shape[1] * page

    perm = jnp.concatenate([jnp.arange(0, dk, 2), jnp.arange(1, dk, 2)])
    head_cols = (jnp.arange(H_A)[:, None] * dk + perm[None, :]).reshape(-1)
    cols = jnp.concatenate([head_cols, d + head_cols, 2 * d + jnp.arange(2 * d)])
    w_in = w_ret_in[0][:, cols].astype(BF16)
    perm_in = (perm[:, None] == jnp.arange(dk)[None, :]).astype(BF16)
    perms = (perm_in, perm_in.T)
    w_out = w_ret_out[0].astype(BF16)
    w_up = w_ffn_up.astype(BF16)
    w_down = w_ffn_down.astype(BF16)
    w_kv = w_kvf[:, :2 * d].astype(BF16)
    w_ft = jnp.zeros((ROWS_BF16, d), BF16).at[:H_B].set(w_kvf[:, 2 * d:].T.astype(BF16))
    b_col = jnp.zeros((ROWS_BF16, 1), F32).at[:H_B, 0].set(b_f.astype(F32))
    w_qg = w_b_qg[0].astype(BF16)
    w_o = w_b_out[0].astype(BF16)
    row = lambda v: v.reshape(1, d).astype(F32)
    g_mix0, g_mix1 = row(norm_mix[0]), row(norm_mix[1])
    g_ffn0, g_ffn1 = row(norm_ffn[0]), row(norm_ffn[1])
    g_kv, g_fin, gn = row(norm_kv), row(norm_final), row(gn_ret[0])
    q_scale = LOG2E * dh ** -0.5

    def trunk_tokens(h1, m_tile):
        bb, tt, _ = h1.shape
        h2 = _ffn_block(h1.reshape(bb * tt, d), g_ffn0, w_up, w_down, 0, tm=m_tile).reshape(bb, tt, d)
        return h2, _projections(h2, g_kv, g_mix1, w_kv, w_ft, b_col, w_qg, tm=m_tile,
                                q_scale=q_scale, c_scale=LOG2E)

    tc = _pick_tile(t, TOKEN_TILE)
    ret_tile = _pick_tile(t, RET_TILE)
    chunk = _pick_tile(ret_tile, RET_CHUNK)
    cos_p, sin_p = _rope_tables(jnp.arange(t), dk // 2)
    h1, s_prompt = _retention_layer(
        x_prompt, None, cos_p, sin_p, g_mix0, w_in, gn, w_out,
        _decay_tables(chunk, chunk, dk), perms, tc=ret_tile, chunk=chunk)
    h2, (k_p, v_p, kb_p, vb_p, lf_p, c_p, q_p, gate_p) = trunk_tokens(h1, tc)
    a_p = _prompt_attention(q_p, kb_p, vb_p, c_p, gate_p, tq=_pick_tile(t, ATTN_TILE),
                            heads=ATTN_HEADS)
    y_prompt = _ffn_block(h2.reshape(b * t, d), g_ffn1, w_up, w_down, 1, tm=tc,
                          attn=(a_p.reshape(b * t, d), w_o), g_final=g_fin).reshape(b, t, d)

    rows = SAMPLE_CHUNK
    cos_s, sin_s = _rope_tables(past + jnp.arange(rows), dk // 2)
    h1s, s_sample = _retention_layer(
        x_sample, state_ret[0], cos_s, sin_s, g_mix0, w_in, gn, w_out,
        _decay_tables(rows, td, dk), perms, tc=rows, chunk=rows, nseq=_pick_tile(bd, SAMPLE_SEQS))
    h1s = h1s.reshape(1, bd * td, d)
    h2s, (k_s, v_s, kb_s, vb_s, lf_s, _, q_s, gate_s) = trunk_tokens(h1s, bd * td)
    per_head = lambda a: a.reshape(bd * td * H_B, dh)
    lf_s_tok = jnp.swapaxes(lf_s[0], 0, 1)
    a_s = _sample_attention(
        page_table, per_head(q_s), per_head(gate_s), per_head(kb_s), per_head(vb_s),
        lf_s[0],
        cache_k.reshape(n_pool, page * H_B, dh), cache_v.reshape(n_pool, page * H_B, dh),
        jnp.swapaxes(cache_logf, 1, 2),
        group=_pick_tile(page_table.shape[1], PAGE_GROUP), t_new=td)
    y_sample = _ffn_block(h2s.reshape(bd * td, d), g_ffn1, w_up, w_down, 1, tm=bd * td,
                          attn=(a_s.reshape(bd * td, d), w_o), g_final=g_fin).reshape(bd, td, d)

    return (y_prompt, y_sample, s_prompt[None],
            k_p.reshape(b, t, H_B, dh), v_p.reshape(b, t, H_B, dh), jnp.swapaxes(lf_p, 1, 2),
            s_sample[None],
            k_s.reshape(bd, td, H_B, dh), v_s.reshape(bd, td, H_B, dh),
            lf_s_tok.reshape(bd, td, H_B))
```

```python
import functools
import math

import jax
import jax.numpy as jnp
from jax import lax
from jax.experimental import pallas as pl
from jax.experimental.pallas import tpu as pltpu

H_A = 4
H_B = 8
RMS_EPS = 1e-6
GN_EPS = 1e-5
ROPE_BASE = 10000.0
LOG2E = math.log2(math.e)
LANES = 128
ROWS_BF16 = 16
N_SLOTS = 3
VMEM_LIMIT_BYTES = 56 * 1024 * 1024

F32 = jnp.float32
BF16 = jnp.bfloat16


def _dot(a, b):
    return jnp.dot(a, b, preferred_element_type=F32)


def _dot_nt(a, b):
    return lax.dot_general(a, b, (((1,), (1,)), ((), ())), preferred_element_type=F32)


def _dot_tn(a, b):
    return lax.dot_general(a, b, (((0,), (0,)), ((), ())), preferred_element_type=F32)


def _rmsnorm(x, g):
    return x * lax.rsqrt(jnp.mean(x * x, axis=-1, keepdims=True) + RMS_EPS) * g


def _sigmoid(x):
    return 1.0 / (1.0 + jnp.exp(-x))


def _log_sigmoid(x):
    return jnp.minimum(x, 0.0) - jnp.log1p(jnp.exp(-jnp.abs(x)))


def _split3(x):
    return tuple(part.astype(BF16) for part in _split3_f32(x))


def _split3_f32(x):
    hi = x.astype(BF16).astype(F32)
    r1 = x - hi
    mid = r1.astype(BF16).astype(F32)
    lo = (r1 - mid).astype(BF16).astype(F32)
    return hi, mid, lo


def _const_spec(shape):
    nd = len(shape)
    return pl.BlockSpec(shape, lambda *_: (0,) * nd, pipeline_mode=pl.Buffered(1))


def _params(*sem):
    return pltpu.CompilerParams(dimension_semantics=sem, vmem_limit_bytes=VMEM_LIMIT_BYTES)


def _permute_rows(p_ref, s):
    p = p_ref[...]
    hi, mid, lo = _split3(s)
    return _dot(p, hi) + _dot(p, mid) + _dot(p, lo)


def _permute_cols(z, p_ref):
    p = p_ref[...]
    hi, mid, lo = _split3(z)
    return _dot(hi, p) + _dot(mid, p) + _dot(lo, p)


def _ret_kernel(*refs, chunk, has_state):
    refs = list(refs)
    x_ref = refs.pop(0)
    s0_ref = refs.pop(0) if has_state else None
    (cos_ref, sin_ref, g_ref, win_ref, gn_ref, wout_ref, intra_ref, qdec_ref, kdec_ref, cdec_ref,
     pcol_ref, prow_ref, h_ref, sfin_ref, s_scr, gate_scr, *pad_scr) = refs
    t = pl.program_id(1)
    nseq, tx, d = x_ref.shape
    tc = gate_scr.shape[0] // nseq
    dk = d // H_A
    half = dk // 2

    @pl.when(t == 0)
    def _():
        for i in range(nseq):
            for h in range(H_A):
                s_scr[i * H_A + h] = s0_ref[i, h] if has_state else jnp.zeros((dk, dk), F32)

    if tx == tc:
        x = x_ref[...].reshape(nseq * tc, d)
    else:
        xpad_scr, = pad_scr
        xpad_scr[...] = jnp.zeros_like(xpad_scr)
        for i in range(nseq):
            xpad_scr[i * tc:i * tc + tx, :] = x_ref[i]
        x = xpad_scr[...]
    hn = _rmsnorm(x, g_ref[...]).astype(BF16)
    cos = cos_ref[...]
    sin = sin_ref[...]

    def rot(z):
        z1, z2 = z[:, :half], z[:, half:]
        return jnp.concatenate([z1 * cos - z2 * sin, z1 * sin + z2 * cos], axis=1)

    for h in range(H_A):
        def proj(j, h=h):
            return _dot(hn, win_ref[:, j * d + h * dk:j * d + (h + 1) * dk])

        q = rot(proj(0)).astype(BF16)
        k = rot(proj(1)) * (dk ** -0.5)
        if has_state:
            q = _dot(q, pcol_ref[...]).astype(BF16)
            k = _permute_cols(k, pcol_ref)
        v = proj(2).astype(BF16)
        g = proj(3)
        for c in range(nseq * tc // chunk):
            r = slice(c * chunk, (c + 1) * chunk)
            st = (c * chunk // tc) * H_A + h
            qc, kc, vc = q[r], k[r], v[r]
            s_old = s_scr[st]
            scores = _dot_nt(qc, kc.astype(BF16)) * intra_ref[h]
            o = _dot(scores.astype(BF16), vc) + _dot(qc, s_old.astype(BF16)) * qdec_ref[h]
            s_scr[st] = s_old * cdec_ref[h] + _dot_tn((kc * kdec_ref[h]).astype(BF16), vc)
            mu = jnp.mean(o, axis=-1, keepdims=True)
            oc = o - mu
            var = jnp.mean(oc * oc, axis=-1, keepdims=True)
            on = oc * lax.rsqrt(var + GN_EPS) * gn_ref[:, h * dk:(h + 1) * dk]
            gc = g[r]
            gate_scr[r, h * dk:(h + 1) * dk] = (gc * _sigmoid(gc) * on).astype(BF16)

    out = x + _dot(gate_scr[...], wout_ref[...])
    if tx == tc:
        h_ref[...] = out.reshape(nseq, tc, d)
    else:
        for i in range(nseq):
            h_ref[i] = out[i * tc:i * tc + tx]

    @pl.when(t == pl.num_programs(1) - 1)
    def _():
        for i in range(nseq):
            for h in range(H_A):
                s_fin = s_scr[i * H_A + h]
                sfin_ref[i, h] = s_fin if has_state else _permute_rows(prow_ref, s_fin)


def _retention_layer(x, s0, cos, sin, g, w_in, gn, w_out, dec, perms, *, tc, chunk, nseq=1):
    b, t, d = x.shape
    dk = d // H_A
    tx = min(t, tc)
    state_spec = pl.BlockSpec((nseq, H_A, dk, dk), lambda i, j: (i, 0, 0, 0))
    tok_spec = pl.BlockSpec((nseq, tx, d), lambda i, j: (i, j, 0))
    pad_scratch = [] if tx == tc else [pltpu.VMEM((nseq * tc, d), F32)]
    cos, sin = jnp.tile(cos, (nseq, 1)), jnp.tile(sin, (nseq, 1))
    rope_spec = pl.BlockSpec((nseq * tc, dk // 2), lambda i, j: (j, 0))
    consts = [g, w_in, gn, w_out, *dec, *perms]
    args = [x] + ([] if s0 is None else [s0]) + [cos, sin] + consts
    specs = ([tok_spec] + ([] if s0 is None else [state_spec]) + [rope_spec, rope_spec]
             + [_const_spec(a.shape) for a in consts])
    return pl.pallas_call(
        functools.partial(_ret_kernel, chunk=chunk, has_state=s0 is not None),
        name="retention_layer",
        grid=(b // nseq, t // tx),
        in_specs=specs,
        out_specs=[tok_spec, state_spec],
        out_shape=[jax.ShapeDtypeStruct((b, t, d), F32),
                   jax.ShapeDtypeStruct((b, H_A, dk, dk), F32)],
        scratch_shapes=[pltpu.VMEM((nseq * H_A, dk, dk), F32), pltpu.VMEM((nseq * tc, d), BF16),
                        *pad_scratch],
        compiler_params=_params("parallel", "arbitrary"),
    )(*args)


def _decay_tables(rows, c_true, dk):
    lg = jnp.log(1.0 - jnp.exp2(-5.0 - jnp.arange(H_A, dtype=F32)))
    i = jnp.arange(rows, dtype=F32)
    diff = i[:, None] - i[None, :]
    intra = jnp.where(diff >= 0, jnp.exp(lg[:, None, None] * jnp.maximum(diff, 0.0)), 0.0)
    qdec = jnp.exp(lg[:, None] * (i[None, :] + 1.0))
    kdec = jnp.exp(lg[:, None] * (c_true - 1.0 - i[None, :]))
    cdec = jnp.exp(lg * c_true)
    return (intra,
            jnp.broadcast_to(qdec[:, :, None], (H_A, rows, dk)),
            jnp.broadcast_to(kdec[:, :, None], (H_A, rows, dk)),
            jnp.broadcast_to(cdec[:, None, None], (H_A, 1, dk)))


def _rope_tables(pos, half):
    inv = 1.0 / (ROPE_BASE ** jnp.linspace(0.0, 1.0, half, dtype=F32))
    ang = pos.astype(F32)[:, None] * inv[None, :]
    return jnp.cos(ang), jnp.sin(ang)


def _ffn_kernel(*refs, ff, has_attn, final_norm):
    refs = list(refs)
    x_ref = refs.pop(0)
    if has_attn:
        a_ref, wo_ref = refs.pop(0), refs.pop(0)
    g_ref, wup_ref, wdown_ref = refs.pop(0), refs.pop(0), refs.pop(0)
    if final_norm:
        gfin_ref = refs.pop(0)
    o_ref, = refs

    x = x_ref[...]
    if has_attn:
        x = x + _dot(a_ref[...].astype(BF16), wo_ref[...])
    hn = _rmsnorm(x, g_ref[...]).astype(BF16)
    gate = _dot(hn, wup_ref[:, :ff])
    up = _dot(hn, wup_ref[:, ff:])
    act = (gate * _sigmoid(gate) * up).astype(BF16)
    y = x + _dot(act, wdown_ref[...])
    if final_norm:
        y = _rmsnorm(y, gfin_ref[...])
    o_ref[...] = y


def _layer_spec(w, layer):
    return pl.BlockSpec((None,) + w.shape[1:], lambda *_: (layer, 0, 0), pipeline_mode=pl.Buffered(1))


def _ffn_block(x, g, w_up, w_down, layer, *, tm, attn=None, g_final=None):
    m, d = x.shape
    ff = w_down.shape[1]
    row = lambda i: (i, 0)
    args, specs = [x], [pl.BlockSpec((tm, d), row)]
    if attn is not None:
        a, w_o = attn
        args += [a, w_o]
        specs += [pl.BlockSpec((tm, d), row), _const_spec(w_o.shape)]
    args += [g, w_up, w_down]
    specs += [_const_spec(g.shape), _layer_spec(w_up, layer), _layer_spec(w_down, layer)]
    if g_final is not None:
        args.append(g_final)
        specs.append(_const_spec(g_final.shape))
    return pl.pallas_call(
        functools.partial(_ffn_kernel, ff=ff, has_attn=attn is not None,
                          final_norm=g_final is not None),
        name="ffn_block",
        grid=(m // tm,),
        in_specs=specs,
        out_specs=pl.BlockSpec((tm, d), row),
        out_shape=jax.ShapeDtypeStruct((m, d), F32),
        compiler_params=_params("parallel"),
    )(*args)


def _proj_kernel(x_ref, gkv_ref, gq_ref, wkv_ref, wft_ref, bf_ref, tri_ref, wqg_ref,
                 k_ref, v_ref, kb_ref, vb_ref, lf_ref, c_ref, q_ref, gate_ref, carry_scr,
                 *, q_scale, c_scale):
    t = pl.program_id(1)
    d = x_ref.shape[2]
    tm = x_ref.shape[1]

    @pl.when(t == 0)
    def _():
        carry_scr[...] = jnp.zeros_like(carry_scr)

    x = x_ref[0]
    xn = x * lax.rsqrt(jnp.mean(x * x, axis=-1, keepdims=True) + RMS_EPS)
    hkv = (xn * gkv_ref[...]).astype(BF16)
    hq = (xn * gq_ref[...]).astype(BF16)

    kv = _dot(hkv, wkv_ref[...])
    k, v = kv[:, :d], kv[:, d:]
    k_ref[0] = k
    v_ref[0] = v
    kb_ref[0] = k.astype(BF16)
    vb_ref[0] = v.astype(BF16)

    qg = _dot(hq, wqg_ref[...])
    q_ref[0] = (qg[:, :d] * q_scale).astype(BF16)
    gate_ref[0] = qg[:, d:]

    lf = _log_sigmoid(_dot_nt(wft_ref[...], hkv) + bf_ref[...])
    lf_ref[0] = lf[:H_B]
    parts = _dot(jnp.concatenate(_split3(lf), axis=0), tri_ref[...])
    n = lf.shape[0]
    c = parts[:n] + parts[n:2 * n] + parts[2 * n:] + carry_scr[:, 0:1]
    c_ref[0] = c[:H_B] * c_scale
    carry_scr[...] = jnp.broadcast_to(c[:, tm - 1:tm], carry_scr.shape)


def _projections(x, g_kv, g_q, w_kv, w_ft, b_f, w_qg, *, tm, q_scale, c_scale):
    b, t, d = x.shape
    tri = (jnp.arange(tm)[:, None] <= jnp.arange(tm)[None, :]).astype(BF16)
    tok = lambda i, j: (i, j, 0)
    hm = lambda i, j: (i, 0, j)
    tok_spec = pl.BlockSpec((1, tm, d), tok)
    hm_spec = pl.BlockSpec((1, H_B, tm), hm)
    return pl.pallas_call(
        functools.partial(_proj_kernel, q_scale=q_scale, c_scale=c_scale),
        name="kv_qg_projections",
        grid=(b, t // tm),
        in_specs=[tok_spec, _const_spec(g_kv.shape), _const_spec(g_q.shape), _const_spec(w_kv.shape),
                  _const_spec(w_ft.shape), _const_spec(b_f.shape), _const_spec(tri.shape),
                  _const_spec(w_qg.shape)],
        out_specs=[tok_spec, tok_spec, tok_spec, tok_spec, hm_spec, hm_spec, tok_spec, tok_spec],
        out_shape=[jax.ShapeDtypeStruct((b, t, d), F32), jax.ShapeDtypeStruct((b, t, d), F32),
                   jax.ShapeDtypeStruct((b, t, d), BF16), jax.ShapeDtypeStruct((b, t, d), BF16),
                   jax.ShapeDtypeStruct((b, H_B, t), F32), jax.ShapeDtypeStruct((b, H_B, t), F32),
                   jax.ShapeDtypeStruct((b, t, d), BF16), jax.ShapeDtypeStruct((b, t, d), F32)],
        scratch_shapes=[pltpu.VMEM((ROWS_BF16, LANES), F32)],
        compiler_params=_params("parallel", "arbitrary"),
    )(x, g_kv, g_q, w_kv, w_ft, b_f, tri, w_qg)


def _attn_kernel(q_ref, k_ref, v_ref, c_ref, gate_ref, o_ref, *scr, dh):
    qi = pl.program_id(2)
    tq = q_ref.shape[1]
    lanes = [slice(h * dh, (h + 1) * dh) for h in range(q_ref.shape[2] // dh)]
    stats = [scr[3 * h:3 * h + 3] for h in range(len(lanes))]
    for m_scr, l_scr, acc_scr in stats:
        m_scr[...] = jnp.full_like(m_scr, -jnp.inf)
        l_scr[...] = jnp.zeros_like(l_scr)
        acc_scr[...] = jnp.zeros_like(acc_scr)

    def block(kj, diagonal):
        start = pl.multiple_of(kj * tq, tq)
        if diagonal:
            row = lax.broadcasted_iota(jnp.int32, (tq, tq), 0)
            col = lax.broadcasted_iota(jnp.int32, (tq, tq), 1)
            visible = col <= row
        for h, ln in enumerate(lanes):
            m_scr, l_scr, acc_scr = stats[h]
            ks = k_ref[0, pl.ds(start, tq), ln]
            vs = v_ref[0, pl.ds(start, tq), ln]
            s = _dot_nt(q_ref[0, :, ln], ks) - c_ref[0, h, pl.ds(kj, 1), :]
            if diagonal:
                s = jnp.where(visible, s, -jnp.inf)
            chunks = [s[:, j * dh:(j + 1) * dh] for j in range(tq // dh)]
            m_old = m_scr[...]
            m_new = jnp.maximum(m_old, jnp.max(functools.reduce(jnp.maximum, chunks),
                                               axis=-1, keepdims=True))
            parts = [jnp.exp2(c - m_new) for c in chunks]
            alpha = jnp.exp2(m_old - m_new)
            m_scr[...] = m_new
            l_scr[...] = alpha * l_scr[...] + jnp.sum(functools.reduce(jnp.add, parts),
                                                      axis=-1, keepdims=True)
            p = jnp.concatenate([part.astype(BF16) for part in parts], axis=1)
            acc_scr[...] = alpha * acc_scr[...] + _dot(p, vs)

    def body(kj, _):
        block(kj, False)
        return 0

    lax.fori_loop(0, qi, body, 0)
    block(qi, True)
    for h, ln in enumerate(lanes):
        _, l_scr, acc_scr = stats[h]
        o_ref[0, :, ln] = (_sigmoid(gate_ref[0, :, ln]) * (acc_scr[...] / l_scr[...])).astype(o_ref.dtype)


def _prompt_attention(q, kb, vb, c, gate, *, tq, heads):
    b, s, d = q.shape
    dh = d // H_B
    nq = s // tq
    c4 = c.reshape(b, H_B, nq, tq)
    qspec = pl.BlockSpec((1, tq, heads * dh), lambda i, h, j: (i, j, h))
    kvspec = pl.BlockSpec((1, s, heads * dh), lambda i, h, j: (i, 0, h))
    return pl.pallas_call(
        functools.partial(_attn_kernel, dh=dh),
        name="prompt_attention",
        grid=(b, H_B // heads, nq),
        in_specs=[qspec, kvspec, kvspec,
                  pl.BlockSpec((1, heads, nq, tq), lambda i, h, j: (i, h, 0, 0)), qspec],
        out_specs=qspec,
        out_shape=jax.ShapeDtypeStruct((b, s, d), BF16),
        scratch_shapes=[pltpu.VMEM((tq, dh), F32)] * (3 * heads),
        compiler_params=_params("parallel", "parallel", "arbitrary"),
    )(q, kb, vb, c4, gate)


def _sample_attn_kernel(pt_ref, q_ref, gate_ref, knew_ref, vnew_ref, lfnew_ref, cum_ref, cumnew_ref,
                        k_hbm, v_hbm, lf_hbm, o_ref, m_scr, l_scr, acc_scr, carry_scr,
                        k_buf, v_buf, lf_buf, k_sem, v_sem, lf_sem, *, group, t_new):
    b = pl.program_id(0)
    p = pl.program_id(1)
    n_steps = pl.num_programs(1)
    nrow = q_ref.shape[0]
    ncol = k_hbm.shape[1]
    stride = H_B
    q = q_ref[...]

    step = b * n_steps + p
    total_steps = pl.num_programs(0) * n_steps

    def page_copies(step_idx, slot):
        bb = step_idx // n_steps
        first = (step_idx % n_steps) * group
        copies = []
        for gi in range(group):
            page = pt_ref[bb, first + gi]
            dst = slot * group + gi
            copies += [pltpu.make_async_copy(k_hbm.at[page], k_buf.at[dst], k_sem.at[slot]),
                       pltpu.make_async_copy(v_hbm.at[page], v_buf.at[dst], v_sem.at[slot]),
                       pltpu.make_async_copy(lf_hbm.at[page], lf_buf.at[dst], lf_sem.at[slot])]
        return copies

    def start_all(copies):
        for n, cp in enumerate(copies):
            cp.start(priority=1 if n % 3 == 1 else 0)

    @pl.when(step == 0)
    def _():
        for ahead in range(N_SLOTS - 1):
            start_all(page_copies(ahead, ahead))

    @pl.when(step + (N_SLOTS - 1) < total_steps)
    def _():
        nxt = step + (N_SLOTS - 1)
        start_all(page_copies(nxt, nxt % N_SLOTS))

    slot = step % N_SLOTS
    for cp in page_copies(step, slot):
        cp.wait()
    k_refs = [k_buf.at[slot * group + gi] for gi in range(group)]
    v_refs = [v_buf.at[slot * group + gi] for gi in range(group)]
    lf_refs = [lf_buf.at[slot * group + gi] for gi in range(group)]

    @pl.when(p == 0)
    def _():
        m_scr[...] = jnp.full_like(m_scr, -jnp.inf)
        l_scr[...] = jnp.zeros_like(l_scr)
        acc_scr[...] = jnp.zeros_like(acc_scr)
        carry_scr[...] = jnp.zeros_like(carry_scr)

    def same_head(n):
        row = lax.broadcasted_iota(jnp.int32, (nrow, n), 0)
        col = lax.broadcasted_iota(jnp.int32, (nrow, n), 1)
        return row % stride == col % stride, row, col

    def update(scores, values):
        m_old = m_scr[:, 0:1]
        m_new = m_old
        for s in scores:
            m_new = jnp.maximum(m_new, jnp.max(s, axis=-1, keepdims=True))
        alpha = jnp.exp2(m_old - m_new)
        l_new = alpha * l_scr[:, 0:1]
        acc = alpha * acc_scr[...]
        for s, v in zip(scores, values):
            pr = jnp.exp2(s - m_new)
            l_new = l_new + jnp.sum(pr, axis=-1, keepdims=True)
            acc = acc + _dot(pr.astype(BF16), v)
        l_scr[...] = jnp.broadcast_to(l_new, l_scr.shape)
        acc_scr[...] = acc
        m_scr[...] = jnp.broadcast_to(m_new, m_scr.shape)

    def running_sum(lf, cum_ref):
        n = lf.shape[0]
        parts = _dot(jnp.concatenate(_split3_f32(lf), axis=0).astype(BF16), cum_ref[...])
        return parts[:n] + parts[n:2 * n] + parts[2 * n:3 * n]

    lf = jnp.concatenate([r[...] for r in lf_refs], axis=0) * LOG2E
    within = running_sum(lf, cum_ref)
    mask, _, _ = same_head(ncol)
    carry = carry_scr[:, 0:1]
    scores, values = [], []
    for gi in range(group):
        w = within[gi * H_B:(gi + 1) * H_B]
        kb = k_refs[gi][...].astype(BF16)
        s = _dot_nt(q, kb) - jnp.tile(carry + w, (t_new, 1))
        scores.append(jnp.where(mask, s, -jnp.inf))
        values.append(v_refs[gi][...].astype(BF16))
        carry = carry + w[:, ncol - 1:ncol]
    carry_scr[...] = jnp.broadcast_to(carry, carry_scr.shape)
    update(scores, values)

    @pl.when(p == pl.num_programs(1) - 1)
    def _():
        n = knew_ref.shape[0]
        x = lfnew_ref[...] * LOG2E
        zeros = jnp.zeros_like(x)
        bias_new = carry + running_sum(jnp.concatenate([x, zeros], axis=0), cumnew_ref)[:H_B]
        mask_new, row, col = same_head(n)
        tok = col // stride
        visible = mask_new & (tok // t_new == b) & (tok % t_new <= row // stride)
        s_new = _dot_nt(q, knew_ref[...]) - jnp.tile(bias_new, (t_new, 1))
        update([jnp.where(visible, s_new, -jnp.inf)], [vnew_ref[...]])
        o_ref[...] = _sigmoid(gate_ref[...]) * (acc_scr[...] / l_scr[:, 0:1])


def _sample_attention(page_table, q, gate, k_new, v_new, lf_new, cache_k, cache_v, cache_lf, *, group, t_new):
    n_rows, dh = q.shape
    nrow = t_new * H_B
    bd = n_rows // nrow
    n_pages = page_table.shape[1]
    ncol = cache_k.shape[1]
    page = cache_lf.shape[2]
    n_tok = lf_new.shape[1]
    key_of = jnp.arange(ncol)[None, :] // H_B
    cum = (jnp.arange(page)[:, None] <= key_of).astype(BF16)
    tok, tok_of = jnp.arange(n_tok)[:, None], jnp.arange(n_tok * H_B)[None, :] // H_B
    cum_new = ((tok <= tok_of) & (tok // t_new == tok_of // t_new)).astype(BF16)
    per_b = pl.BlockSpec((nrow, dh), lambda i, j, pt: (i, 0))
    whole = lambda a: pl.BlockSpec(a.shape, lambda i, j, pt: (0,) * a.ndim)

    hbm = pl.BlockSpec(memory_space=pl.ANY)
    in_specs = [per_b, per_b, whole(k_new), whole(v_new), whole(lf_new), whole(cum), whole(cum_new),
                hbm, hbm, hbm]
    n_buf = N_SLOTS * group
    grid_spec = pltpu.PrefetchScalarGridSpec(
        num_scalar_prefetch=1,
        grid=(bd, n_pages // group),
        in_specs=in_specs,
        out_specs=per_b,
        scratch_shapes=[pltpu.VMEM((nrow, LANES), F32), pltpu.VMEM((nrow, LANES), F32),
                        pltpu.VMEM((nrow, dh), F32), pltpu.VMEM((H_B, LANES), F32),
                        pltpu.VMEM((n_buf, ncol, dh), cache_k.dtype),
                        pltpu.VMEM((n_buf, ncol, dh), cache_v.dtype),
                        pltpu.VMEM((n_buf, H_B, page), cache_lf.dtype),
                        pltpu.SemaphoreType.DMA((N_SLOTS,)), pltpu.SemaphoreType.DMA((N_SLOTS,)),
                        pltpu.SemaphoreType.DMA((N_SLOTS,))],
    )
    return pl.pallas_call(
        functools.partial(_sample_attn_kernel, group=group, t_new=t_new),
        name="sample_attention",
        grid_spec=grid_spec,
        out_shape=jax.ShapeDtypeStruct((n_rows, dh), F32),
        compiler_params=_params("arbitrary", "arbitrary"),
    )(page_table, q, gate, k_new, v_new, lf_new, cum, cum_new, cache_k, cache_v, cache_lf)


TOKEN_TILE = 512
RET_TILE = 1024
RET_CHUNK = 256
ATTN_TILE = 512
ATTN_HEADS = H_B
SAMPLE_CHUNK = LANES
SAMPLE_SEQS = 4
PAGE_GROUP = 8


def _pick_tile(n, target):
    t = min(n, target)
    while n % t:
        t //= 2
    return t


def kernel(x_prompt, x_sample, state_ret, cache_k, cache_v, cache_logf, page_table,
           norm_mix, w_ret_in, gn_ret, w_ret_out, norm_kv, w_kvf, b_f,
           w_b_qg, w_b_out, norm_ffn, w_ffn_up, w_ffn_down, norm_final):
    b, t, d = x_prompt.shape
    bd, td, _ = x_sample.shape
    dk = d // H_A
    dh = d // H_B
    n_pool, page = cache_k.shape[0], cache_k.shape[1]
    past = page_table.shape[1] * page

    perm = jnp.concatenate([jnp.arange(0, dk, 2), jnp.arange(1, dk, 2)])
    head_cols = (jnp.arange(H_A)[:, None] * dk + perm[None, :]).reshape(-1)
    cols = jnp.concatenate([head_cols, d + head_cols, 2 * d + jnp.arange(2 * d)])
    w_in = w_ret_in[0][:, cols].astype(BF16)
    perm_in = (perm[:, None] == jnp.arange(dk)[None, :]).astype(BF16)
    perms = (perm_in, perm_in.T)
    w_out = w_ret_out[0].astype(BF16)
    w_up = w_ffn_up.astype(BF16)
    w_down = w_ffn_down.astype(BF16)
    w_kv = w_kvf[:, :2 * d].astype(BF16)
    w_ft = jnp.zeros((ROWS_BF16, d), BF16).at[:H_B].set(w_kvf[:, 2 * d:].T.astype(BF16))
    b_col = jnp.zeros((ROWS_BF16, 1), F32).at[:H_B, 0].set(b_f.astype(F32))
    w_qg = w_b_qg[0].astype(BF16)
    w_o = w_b_out[0].astype(BF16)
    row = lambda v: v.reshape(1, d).astype(F32)
    g_mix0, g_mix1 = row(norm_mix[0]), row(norm_mix[1])
    g_ffn0, g_ffn1 = row(norm_ffn[0]), row(norm_ffn[1])
    g_kv, g_fin, gn = row(norm_kv), row(norm_final), row(gn_ret[0])
    q_scale = LOG2E * dh ** -0.5

    def trunk_tokens(h1, m_tile):
        bb, tt, _ = h1.shape
        h2 = _ffn_block(h1.reshape(bb * tt, d), g_ffn0, w_up, w_down, 0, tm=m_tile).reshape(bb, tt, d)
        return h2, _projections(h2, g_kv, g_mix1, w_kv, w_ft, b_col, w_qg, tm=m_tile,
                                q_scale=q_scale, c_scale=LOG2E)

    tc = _pick_tile(t, TOKEN_TILE)
    ret_tile = _pick_tile(t, RET_TILE)
    chunk = _pick_tile(ret_tile, RET_CHUNK)
    cos_p, sin_p = _rope_tables(jnp.arange(t), dk // 2)
    h1, s_prompt = _retention_layer(
        x_prompt, None, cos_p, sin_p, g_mix0, w_in, gn, w_out,
        _decay_tables(chunk, chunk, dk), perms, tc=ret_tile, chunk=chunk)
    h2, (k_p, v_p, kb_p, vb_p, lf_p, c_p, q_p, gate_p) = trunk_tokens(h1, tc)
    a_p = _prompt_attention(q_p, kb_p, vb_p, c_p, gate_p, tq=_pick_tile(t, ATTN_TILE),
                            heads=ATTN_HEADS)
    y_prompt = _ffn_block(h2.reshape(b * t, d), g_ffn1, w_up, w_down, 1, tm=tc,
                          attn=(a_p.reshape(b * t, d), w_o), g_final=g_fin).reshape(b, t, d)

    rows = SAMPLE_CHUNK
    cos_s, sin_s = _rope_tables(past + jnp.arange(rows), dk // 2)
    h1s, s_sample = _retention_layer(
        x_sample, state_ret[0], cos_s, sin_s, g_mix0, w_in, gn, w_out,
        _decay_tables(rows, td, dk), perms, tc=rows, chunk=rows, nseq=_pick_tile(bd, SAMPLE_SEQS))
    h1s = h1s.reshape(1, bd * td, d)
    h2s, (k_s, v_s, kb_s, vb_s, lf_s, _, q_s, gate_s) = trunk_tokens(h1s, bd * td)
    per_head = lambda a: a.reshape(bd * td * H_B, dh)
    lf_s_tok = jnp.swapaxes(lf_s[0], 0, 1)
    a_s = _sample_attention(
        page_table, per_head(q_s), per_head(gate_s), per_head(kb_s), per_head(vb_s),
        lf_s[0],
        cache_k.reshape(n_pool, page * H_B, dh), cache_v.reshape(n_pool, page * H_B, dh),
        jnp.swapaxes(cache_logf, 1, 2),
        group=_pick_tile(page_table.shape[1], PAGE_GROUP), t_new=td)
    y_sample = _ffn_block(h2s.reshape(bd * td, d), g_ffn1, w_up, w_down, 1, tm=bd * td,
                          attn=(a_s.reshape(bd * td, d), w_o), g_final=g_fin).reshape(bd, td, d)

    return (y_prompt, y_sample, s_prompt[None],
            k_p.reshape(b, t, H_B, dh), v_p.reshape(b, t, H_B, dh), jnp.swapaxes(lf_p, 1, 2),
            s_sample[None],
            k_s.reshape(bd, td, H_B, dh), v_s.reshape(bd, td, H_B, dh),
            lf_s_tok.reshape(bd, td, H_B))
```

```python
import functools
import math

import jax
import jax.numpy as jnp
from jax import lax
from jax.experimental import pallas as pl
from jax.experimental.pallas import tpu as pltpu

H_A = 4
H_B = 8
RMS_EPS = 1e-6
GN_EPS = 1e-5
ROPE_BASE = 10000.0
LOG2E = math.log2(math.e)
LANES = 128
ROWS_BF16 = 16
N_SLOTS = 3
VMEM_LIMIT_BYTES = 56 * 1024 * 1024

F32 = jnp.float32
BF16 = jnp.bfloat16


def _dot(a, b):
    return jnp.dot(a, b, preferred_element_type=F32)


def _dot_nt(a, b):
    return lax.dot_general(a, b, (((1,), (1,)), ((), ())), preferred_element_type=F32)


def _dot_tn(a, b):
    return lax.dot_general(a, b, (((0,), (0,)), ((), ())), preferred_element_type=F32)


def _rmsnorm(x, g):
    return x * lax.rsqrt(jnp.mean(x * x, axis=-1, keepdims=True) + RMS_EPS) * g


def _sigmoid(x):
    return 1.0 / (1.0 + jnp.exp(-x))


def _log_sigmoid(x):
    return jnp.minimum(x, 0.0) - jnp.log1p(jnp.exp(-jnp.abs(x)))


def _split3(x):
    return tuple(part.astype(BF16) for part in _split3_f32(x))


def _split3_f32(x):
    hi = x.astype(BF16).astype(F32)
    r1 = x - hi
    mid = r1.astype(BF16).astype(F32)
    lo = (r1 - mid).astype(BF16).astype(F32)
    return hi, mid, lo


def _const_spec(shape):
    nd = len(shape)
    return pl.BlockSpec(shape, lambda *_: (0,) * nd, pipeline_mode=pl.Buffered(1))


def _params(*sem):
    return pltpu.CompilerParams(dimension_semantics=sem, vmem_limit_bytes=VMEM_LIMIT_BYTES)


def _permute_rows(p_ref, s):
    p = p_ref[...]
    hi, mid, lo = _split3(s)
    return _dot(p, hi) + _dot(p, mid) + _dot(p, lo)


def _permute_cols(z, p_ref):
    p = p_ref[...]
    hi, mid, lo = _split3(z)
    return _dot(hi, p) + _dot(mid, p) + _dot(lo, p)


def _ret_kernel(*refs, chunk, has_state):
    refs = list(refs)
    x_ref = refs.pop(0)
    s0_ref = refs.pop(0) if has_state else None
    (cos_ref, sin_ref, g_ref, win_ref, gn_ref, wout_ref, intra_ref, qdec_ref, kdec_ref, cdec_ref,
     pcol_ref, prow_ref, h_ref, sfin_ref, s_scr, gate_scr, *pad_scr) = refs
    t = pl.program_id(1)
    nseq, tx, d = x_ref.shape
    tc = gate_scr.shape[0] // nseq
    dk = d // H_A
    half = dk // 2

    @pl.when(t == 0)
    def _():
        for i in range(nseq):
            for h in range(H_A):
                s_scr[i * H_A + h] = s0_ref[i, h] if has_state else jnp.zeros((dk, dk), F32)

    if tx == tc:
        x = x_ref[...].reshape(nseq * tc, d)
    else:
        xpad_scr, = pad_scr
        xpad_scr[...] = jnp.zeros_like(xpad_scr)
        for i in range(nseq):
            xpad_scr[i * tc:i * tc + tx, :] = x_ref[i]
        x = xpad_scr[...]
    hn = _rmsnorm(x, g_ref[...]).astype(BF16)
    cos = cos_ref[...]
    sin = sin_ref[...]

    def rot(z):
        z1, z2 = z[:, :half], z[:, half:]
        return jnp.concatenate([z1 * cos - z2 * sin, z1 * sin + z2 * cos], axis=1)

    for h in range(H_A):
        def proj(j, h=h):
            return _dot(hn, win_ref[:, j * d + h * dk:j * d + (h + 1) * dk])

        q = rot(proj(0)).astype(BF16)
        k = rot(proj(1)) * (dk ** -0.5)
        if has_state:
            q = _dot(q, pcol_ref[...]).astype(BF16)
            k = _permute_cols(k, pcol_ref)
        v = proj(2).astype(BF16)
        g = proj(3)
        for c in range(nseq * tc // chunk):
            r = slice(c * chunk, (c + 1) * chunk)
            st = (c * chunk // tc) * H_A + h
            qc, kc, vc = q[r], k[r], v[r]
            s_old = s_scr[st]
            scores = _dot_nt(qc, kc.astype(BF16)) * intra_ref[h]
            o = _dot(scores.astype(BF16), vc) + _dot(qc, s_old.astype(BF16)) * qdec_ref[h]
            s_scr[st] = s_old * cdec_ref[h] + _dot_tn((kc * kdec_ref[h]).astype(BF16), vc)
            mu = jnp.mean(o, axis=-1, keepdims=True)
            oc = o - mu
            var = jnp.mean(oc * oc, axis=-1, keepdims=True)
            on = oc * lax.rsqrt(var + GN_EPS) * gn_ref[:, h * dk:(h + 1) * dk]
            gc = g[r]
            gate_scr[r, h * dk:(h + 1) * dk] = (gc * _sigmoid(gc) * on).astype(BF16)

    out = x + _dot(gate_scr[...], wout_ref[...])
    if tx == tc:
        h_ref[...] = out.reshape(nseq, tc, d)
    else:
        for i in range(nseq):
            h_ref[i] = out[i * tc:i * tc + tx]

    @pl.when(t == pl.num_programs(1) - 1)
    def _():
        for i in range(nseq):
            for h in range(H_A):
                s_fin = s_scr[i * H_A + h]
                sfin_ref[i, h] = s_fin if has_state else _permute_rows(prow_ref, s_fin)


def _retention_layer(x, s0, cos, sin, g, w_in, gn, w_out, dec, perms, *, tc, chunk, nseq=1):
    b, t, d = x.shape
    dk = d // H_A
    tx = min(t, tc)
    state_spec = pl.BlockSpec((nseq, H_A, dk, dk), lambda i, j: (i, 0, 0, 0))
    tok_spec = pl.BlockSpec((nseq, tx, d), lambda i, j: (i, j, 0))
    pad_scratch = [] if tx == tc else [pltpu.VMEM((nseq * tc, d), F32)]
    cos, sin = jnp.tile(cos, (nseq, 1)), jnp.tile(sin, (nseq, 1))
    rope_spec = pl.BlockSpec((nseq * tc, dk // 2), lambda i, j: (j, 0))
    consts = [g, w_in, gn, w_out, *dec, *perms]
    args = [x] + ([] if s0 is None else [s0]) + [cos, sin] + consts
    specs = ([tok_spec] + ([] if s0 is None else [state_spec]) + [rope_spec, rope_spec]
             + [_const_spec(a.shape) for a in consts])
    return pl.pallas_call(
        functools.partial(_ret_kernel, chunk=chunk, has_state=s0 is not None),
        name="retention_layer",
        grid=(b // nseq, t // tx),
        in_specs=specs,
        out_specs=[tok_spec, state_spec],
        out_shape=[jax.ShapeDtypeStruct((b, t, d), F32),
                   jax.ShapeDtypeStruct((b, H_A, dk, dk), F32)],
        scratch_shapes=[pltpu.VMEM((nseq * H_A, dk, dk), F32), pltpu.VMEM((nseq * tc, d), BF16),
                        *pad_scratch],
        compiler_params=_params("parallel", "arbitrary"),
    )(*args)


def _decay_tables(rows, c_true, dk):
    lg = jnp.log(1.0 - jnp.exp2(-5.0 - jnp.arange(H_A, dtype=F32)))
    i = jnp.arange(rows, dtype=F32)
    diff = i[:, None] - i[None, :]
    intra = jnp.where(diff >= 0, jnp.exp(lg[:, None, None] * jnp.maximum(diff, 0.0)), 0.0)
    qdec = jnp.exp(lg[:, None] * (i[None, :] + 1.0))
    kdec = jnp.exp(lg[:, None] * (c_true - 1.0 - i[None, :]))
    cdec = jnp.exp(lg * c_true)
    return (intra,
            jnp.broadcast_to(qdec[:, :, None], (H_A, rows, dk)),
            jnp.broadcast_to(kdec[:, :, None], (H_A, rows, dk)),
            jnp.broadcast_to(cdec[:, None, None], (H_A, 1, dk)))


def _rope_tables(pos, half):
    inv = 1.0 / (ROPE_BASE ** jnp.linspace(0.0, 1.0, half, dtype=F32))
    ang = pos.astype(F32)[:, None] * inv[None, :]
    return jnp.cos(ang), jnp.sin(ang)


def _ffn_kernel(*refs, ff, has_attn, final_norm):
    refs = list(refs)
    x_ref = refs.pop(0)
    if has_attn:
        a_ref, wo_ref = refs.pop(0), refs.pop(0)
    g_ref, wup_ref, wdown_ref = refs.pop(0), refs.pop(0), refs.pop(0)
    if final_norm:
        gfin_ref = refs.pop(0)
    o_ref, = refs

    x = x_ref[...]
    if has_attn:
        x = x + _dot(a_ref[...].astype(BF16), wo_ref[...])
    hn = _rmsnorm(x, g_ref[...]).astype(BF16)
    gate = _dot(hn, wup_ref[:, :ff])
    up = _dot(hn, wup_ref[:, ff:])
    act = (gate * _sigmoid(gate) * up).astype(BF16)
    y = x + _dot(act, wdown_ref[...])
    if final_norm:
        y = _rmsnorm(y, gfin_ref[...])
    o_ref[...] = y


def _layer_spec(w, layer):
    return pl.BlockSpec((None,) + w.shape[1:], lambda *_: (layer, 0, 0), pipeline_mode=pl.Buffered(1))


def _ffn_block(x, g, w_up, w_down, layer, *, tm, attn=None, g_final=None):
    m, d = x.shape
    ff = w_down.shape[1]
    row = lambda i: (i, 0)
    args, specs = [x], [pl.BlockSpec((tm, d), row)]
    if attn is not None:
        a, w_o = attn
        args += [a, w_o]
        specs += [pl.BlockSpec((tm, d), row), _const_spec(w_o.shape)]
    args += [g, w_up, w_down]
    specs += [_const_spec(g.shape), _layer_spec(w_up, layer), _layer_spec(w_down, layer)]
    if g_final is not None:
        args.append(g_final)
        specs.append(_const_spec(g_final.shape))
    return pl.pallas_call(
        functools.partial(_ffn_kernel, ff=ff, has_attn=attn is not None,
                          final_norm=g_final is not None),
        name="ffn_block",
        grid=(m // tm,),
        in_specs=specs,
        out_specs=pl.BlockSpec((tm, d), row),
        out_shape=jax.ShapeDtypeStruct((m, d), F32),
        compiler_params=_params("parallel"),
    )(*args)


def _proj_kernel(x_ref, gkv_ref, gq_ref, wkv_ref, wft_ref, bf_ref, tri_ref, wqg_ref,
                 k_ref, v_ref, kb_ref, vb_ref, lf_ref, c_ref, q_ref, gate_ref, carry_scr,
                 *, q_scale, c_scale):
    t = pl.program_id(1)
    d = x_ref.shape[2]
    tm = x_ref.shape[1]

    @pl.when(t == 0)
    def _():
        carry_scr[...] = jnp.zeros_like(carry_scr)

    x = x_ref[0]
    xn = x * lax.rsqrt(jnp.mean(x * x, axis=-1, keepdims=True) + RMS_EPS)
    hkv = (xn * gkv_ref[...]).astype(BF16)
    hq = (xn * gq_ref[...]).astype(BF16)

    kv = _dot(hkv, wkv_ref[...])
    k, v = kv[:, :d], kv[:, d:]
    k_ref[0] = k
    v_ref[0] = v
    kb_ref[0] = k.astype(BF16)
    vb_ref[0] = v.astype(BF16)

    qg = _dot(hq, wqg_ref[...])
    q_ref[0] = (qg[:, :d] * q_scale).astype(BF16)
    gate_ref[0] = qg[:, d:]

    lf = _log_sigmoid(_dot_nt(wft_ref[...], hkv) + bf_ref[...])
    lf_ref[0] = lf[:H_B]
    parts = _dot(jnp.concatenate(_split3(lf), axis=0), tri_ref[...])
    n = lf.shape[0]
    c = parts[:n] + parts[n:2 * n] + parts[2 * n:] + carry_scr[:, 0:1]
    c_ref[0] = c[:H_B] * c_scale
    carry_scr[...] = jnp.broadcast_to(c[:, tm - 1:tm], carry_scr.shape)


def _projections(x, g_kv, g_q, w_kv, w_ft, b_f, w_qg, *, tm, q_scale, c_scale):
    b, t, d = x.shape
    tri = (jnp.arange(tm)[:, None] <= jnp.arange(tm)[None, :]).astype(BF16)
    tok = lambda i, j: (i, j, 0)
    hm = lambda i, j: (i, 0, j)
    tok_spec = pl.BlockSpec((1, tm, d), tok)
    hm_spec = pl.BlockSpec((1, H_B, tm), hm)
    return pl.pallas_call(
        functools.partial(_proj_kernel, q_scale=q_scale, c_scale=c_scale),
        name="kv_qg_projections",
        grid=(b, t // tm),
        in_specs=[tok_spec, _const_spec(g_kv.shape), _const_spec(g_q.shape), _const_spec(w_kv.shape),
                  _const_spec(w_ft.shape), _const_spec(b_f.shape), _const_spec(tri.shape),
                  _const_spec(w_qg.shape)],
        out_specs=[tok_spec, tok_spec, tok_spec, tok_spec, hm_spec, hm_spec, tok_spec, tok_spec],
        out_shape=[jax.ShapeDtypeStruct((b, t, d), F32), jax.ShapeDtypeStruct((b, t, d), F32),
                   jax.ShapeDtypeStruct((b, t, d), BF16), jax.ShapeDtypeStruct((b, t, d), BF16),
                   jax.ShapeDtypeStruct((b, H_B, t), F32), jax.ShapeDtypeStruct((b, H_B, t), F32),
                   jax.ShapeDtypeStruct((b, t, d), BF16), jax.ShapeDtypeStruct((b, t, d), F32)],
        scratch_shapes=[pltpu.VMEM((ROWS_BF16, LANES), F32)],
        compiler_params=_params("parallel", "arbitrary"),
    )(x, g_kv, g_q, w_kv, w_ft, b_f, tri, w_qg)


def _attn_kernel(q_ref, k_ref, v_ref, c_ref, gate_ref, o_ref, *scr, dh):
    qi = pl.program_id(2)
    tq = q_ref.shape[1]
    lanes = [slice(h * dh, (h + 1) * dh) for h in range(q_ref.shape[2] // dh)]
    stats = [scr[3 * h:3 * h + 3] for h in range(len(lanes))]
    for m_scr, l_scr, acc_scr in stats:
        m_scr[...] = jnp.full_like(m_scr, -jnp.inf)
        l_scr[...] = jnp.zeros_like(l_scr)
        acc_scr[...] = jnp.zeros_like(acc_scr)

    def block(kj, diagonal):
        start = pl.multiple_of(kj * tq, tq)
        if diagonal:
            row = lax.broadcasted_iota(jnp.int32, (tq, tq), 0)
            col = lax.broadcasted_iota(jnp.int32, (tq, tq), 1)
            visible = col <= row
        for h, ln in enumerate(lanes):
            m_scr, l_scr, acc_scr = stats[h]
            ks = k_ref[0, pl.ds(start, tq), ln]
            vs = v_ref[0, pl.ds(start, tq), ln]
            s = _dot_nt(q_ref[0, :, ln], ks) - c_ref[0, h, pl.ds(kj, 1), :]
            if diagonal:
                s = jnp.where(visible, s, -jnp.inf)
            chunks = [s[:, j * dh:(j + 1) * dh] for j in range(tq // dh)]
            m_old = m_scr[...]
            m_new = jnp.maximum(m_old, jnp.max(functools.reduce(jnp.maximum, chunks),
                                               axis=-1, keepdims=True))
            parts = [jnp.exp2(c - m_new) for c in chunks]
            alpha = jnp.exp2(m_old - m_new)
            m_scr[...] = m_new
            p = jnp.concatenate([part.astype(BF16) for part in parts], axis=1)
            pv = _dot(p, jnp.concatenate([vs, jnp.ones((tq, dh), BF16)], axis=1))
            l_scr[...] = alpha * l_scr[...] + pv[:, dh:]
            acc_scr[...] = alpha * acc_scr[...] + pv[:, :dh]

    def body(kj, _):
        block(kj, False)
        return 0

    lax.fori_loop(0, qi, body, 0)
    block(qi, True)
    for h, ln in enumerate(lanes):
        _, l_scr, acc_scr = stats[h]
        o_ref[0, :, ln] = (_sigmoid(gate_ref[0, :, ln]) * (acc_scr[...] / l_scr[...])).astype(o_ref.dtype)


def _prompt_attention(q, kb, vb, c, gate, *, tq, heads):
    b, s, d = q.shape
    dh = d // H_B
    nq = s // tq
    c4 = c.reshape(b, H_B, nq, tq)
    qspec = pl.BlockSpec((1, tq, heads * dh), lambda i, h, j: (i, j, h))
    kvspec = pl.BlockSpec((1, s, heads * dh), lambda i, h, j: (i, 0, h))
    return pl.pallas_call(
        functools.partial(_attn_kernel, dh=dh),
        name="prompt_attention",
        grid=(b, H_B // heads, nq),
        in_specs=[qspec, kvspec, kvspec,
                  pl.BlockSpec((1, heads, nq, tq), lambda i, h, j: (i, h, 0, 0)), qspec],
        out_specs=qspec,
        out_shape=jax.ShapeDtypeStruct((b, s, d), BF16),
        scratch_shapes=[pltpu.VMEM((tq, dh), F32)] * (3 * heads),
        compiler_params=_params("parallel", "parallel", "arbitrary"),
    )(q, kb, vb, c4, gate)


def _sample_attn_kernel(pt_ref, q_ref, gate_ref, knew_ref, vnew_ref, lfnew_ref, cum_ref, cumnew_ref,
                        k_hbm, v_hbm, lf_hbm, o_ref, m_scr, l_scr, acc_scr, carry_scr,
                        k_buf, v_buf, lf_buf, k_sem, v_sem, lf_sem, *, group, t_new):
    b = pl.program_id(0)
    p = pl.program_id(1)
    n_steps = pl.num_programs(1)
    nrow = q_ref.shape[0]
    ncol = k_hbm.shape[1]
    stride = H_B
    q = q_ref[...]

    step = b * n_steps + p
    total_steps = pl.num_programs(0) * n_steps

    def page_copies(step_idx, slot):
        bb = step_idx // n_steps
        first = (step_idx % n_steps) * group
        copies = []
        for gi in range(group):
            page = pt_ref[bb, first + gi]
            dst = slot * group + gi
            copies += [pltpu.make_async_copy(k_hbm.at[page], k_buf.at[dst], k_sem.at[slot]),
                       pltpu.make_async_copy(v_hbm.at[page], v_buf.at[dst], v_sem.at[slot]),
                       pltpu.make_async_copy(lf_hbm.at[page], lf_buf.at[dst], lf_sem.at[slot])]
        return copies

    @pl.when(step == 0)
    def _():
        for ahead in range(N_SLOTS - 1):
            for cp in page_copies(ahead, ahead):
                cp.start()

    @pl.when(step + (N_SLOTS - 1) < total_steps)
    def _():
        nxt = step + (N_SLOTS - 1)
        for cp in page_copies(nxt, nxt % N_SLOTS):
            cp.start()

    slot = step % N_SLOTS
    for cp in page_copies(step, slot):
        cp.wait()
    k_refs = [k_buf.at[slot * group + gi] for gi in range(group)]
    v_refs = [v_buf.at[slot * group + gi] for gi in range(group)]
    lf_refs = [lf_buf.at[slot * group + gi] for gi in range(group)]

    @pl.when(p == 0)
    def _():
        m_scr[...] = jnp.full_like(m_scr, -jnp.inf)
        l_scr[...] = jnp.zeros_like(l_scr)
        acc_scr[...] = jnp.zeros_like(acc_scr)
        carry_scr[...] = jnp.zeros_like(carry_scr)

    def same_head(n):
        row = lax.broadcasted_iota(jnp.int32, (nrow, n), 0)
        col = lax.broadcasted_iota(jnp.int32, (nrow, n), 1)
        return row % stride == col % stride, row, col

    def update(scores, values):
        m_old = m_scr[:, 0:1]
        m_new = m_old
        for s in scores:
            m_new = jnp.maximum(m_new, jnp.max(s, axis=-1, keepdims=True))
        alpha = jnp.exp2(m_old - m_new)
        l_new = alpha * l_scr[:, 0:1]
        acc = alpha * acc_scr[...]
        for s, v in zip(scores, values):
            pr = jnp.exp2(s - m_new)
            l_new = l_new + jnp.sum(pr, axis=-1, keepdims=True)
            acc = acc + _dot(pr.astype(BF16), v)
        l_scr[...] = jnp.broadcast_to(l_new, l_scr.shape)
        acc_scr[...] = acc
        m_scr[...] = jnp.broadcast_to(m_new, m_scr.shape)

    def running_sum(lf, cum_ref):
        n = lf.shape[0]
        parts = _dot(jnp.concatenate(_split3_f32(lf), axis=0).astype(BF16), cum_ref[...])
        return parts[:n] + parts[n:2 * n] + parts[2 * n:3 * n]

    lf = jnp.concatenate([r[...] for r in lf_refs], axis=0) * LOG2E
    within = running_sum(lf, cum_ref)
    mask, _, _ = same_head(ncol)
    carry = carry_scr[:, 0:1]
    scores, values = [], []
    for gi in range(group):
        w = within[gi * H_B:(gi + 1) * H_B]
        kb = k_refs[gi][...].astype(BF16)
        s = _dot_nt(q, kb) - jnp.tile(carry + w, (t_new, 1))
        scores.append(jnp.where(mask, s, -jnp.inf))
        values.append(v_refs[gi][...].astype(BF16))
        carry = carry + w[:, ncol - 1:ncol]
    carry_scr[...] = jnp.broadcast_to(carry, carry_scr.shape)
    update(scores, values)

    @pl.when(p == pl.num_programs(1) - 1)
    def _():
        n = knew_ref.shape[0]
        x = lfnew_ref[...] * LOG2E
        zeros = jnp.zeros_like(x)
        bias_new = carry + running_sum(jnp.concatenate([x, zeros], axis=0), cumnew_ref)[:H_B]
        mask_new, row, col = same_head(n)
        tok = col // stride
        visible = mask_new & (tok // t_new == b) & (tok % t_new <= row // stride)
        s_new = _dot_nt(q, knew_ref[...]) - jnp.tile(bias_new, (t_new, 1))
        update([jnp.where(visible, s_new, -jnp.inf)], [vnew_ref[...]])
        o_ref[...] = _sigmoid(gate_ref[...]) * (acc_scr[...] / l_scr[:, 0:1])


def _sample_attention(page_table, q, gate, k_new, v_new, lf_new, cache_k, cache_v, cache_lf, *, group, t_new):
    n_rows, dh = q.shape
    nrow = t_new * H_B
    bd = n_rows // nrow
    n_pages = page_table.shape[1]
    ncol = cache_k.shape[1]
    page = cache_lf.shape[2]
    n_tok = lf_new.shape[1]
    key_of = jnp.arange(ncol)[None, :] // H_B
    cum = (jnp.arange(page)[:, None] <= key_of).astype(BF16)
    tok, tok_of = jnp.arange(n_tok)[:, None], jnp.arange(n_tok * H_B)[None, :] // H_B
    cum_new = ((tok <= tok_of) & (tok // t_new == tok_of // t_new)).astype(BF16)
    per_b = pl.BlockSpec((nrow, dh), lambda i, j, pt: (i, 0))
    whole = lambda a: pl.BlockSpec(a.shape, lambda i, j, pt: (0,) * a.ndim)

    hbm = pl.BlockSpec(memory_space=pl.ANY)
    in_specs = [per_b, per_b, whole(k_new), whole(v_new), whole(lf_new), whole(cum), whole(cum_new),
                hbm, hbm, hbm]
    n_buf = N_SLOTS * group
    grid_spec = pltpu.PrefetchScalarGridSpec(
        num_scalar_prefetch=1,
        grid=(bd, n_pages // group),
        in_specs=in_specs,
        out_specs=per_b,
        scratch_shapes=[pltpu.VMEM((nrow, LANES), F32), pltpu.VMEM((nrow, LANES), F32),
                        pltpu.VMEM((nrow, dh), F32), pltpu.VMEM((H_B, LANES), F32),
                        pltpu.VMEM((n_buf, ncol, dh), cache_k.dtype),
                        pltpu.VMEM((n_buf, ncol, dh), cache_v.dtype),
                        pltpu.VMEM((n_buf, H_B, page), cache_lf.dtype),
                        pltpu.SemaphoreType.DMA((N_SLOTS,)), pltpu.SemaphoreType.DMA((N_SLOTS,)),
                        pltpu.SemaphoreType.DMA((N_SLOTS,))],
    )
    return pl.pallas_call(
        functools.partial(_sample_attn_kernel, group=group, t_new=t_new),
        name="sample_attention",
        grid_spec=grid_spec,
        out_shape=jax.ShapeDtypeStruct((n_rows, dh), F32),
        compiler_params=_params("arbitrary", "arbitrary"),
    )(page_table, q, gate, k_new, v_new, lf_new, cum, cum_new, cache_k, cache_v, cache_lf)


TOKEN_TILE = 512
RET_TILE = 1024
RET_CHUNK = 256
ATTN_TILE = 512
ATTN_HEADS = H_B
SAMPLE_CHUNK = LANES
SAMPLE_SEQS = 4
PAGE_GROUP = 8


def _pick_tile(n, target):
    t = min(n, target)
    while n % t:
        t //= 2
    return t


def kernel(x_prompt, x_sample, state_ret, cache_k, cache_v, cache_logf, page_table,
           norm_mix, w_ret_in, gn_ret, w_ret_out, norm_kv, w_kvf, b_f,
           w_b_qg, w_b_out, norm_ffn, w_ffn_up, w_ffn_down, norm_final):
    b, t, d = x_prompt.shape
    bd, td, _ = x_sample.shape
    dk = d // H_A
    dh = d // H_B
    n_pool, page = cache_k.shape[0], cache_k.shape[1]
    past = page_table.shape[1] * page

    perm = jnp.concatenate([jnp.arange(0, dk, 2), jnp.arange(1, dk, 2)])
    head_cols = (jnp.arange(H_A)[:, None] * dk + perm[None, :]).reshape(-1)
    cols = jnp.concatenate([head_cols, d + head_cols, 2 * d + jnp.arange(2 * d)])
    w_in = w_ret_in[0][:, cols].astype(BF16)
    perm_in = (perm[:, None] == jnp.arange(dk)[None, :]).astype(BF16)
    perms = (perm_in, perm_in.T)
    w_out = w_ret_out[0].astype(BF16)
    w_up = w_ffn_up.astype(BF16)
    w_down = w_ffn_down.astype(BF16)
    w_kv = w_kvf[:, :2 * d].astype(BF16)
    w_ft = jnp.zeros((ROWS_BF16, d), BF16).at[:H_B].set(w_kvf[:, 2 * d:].T.astype(BF16))
    b_col = jnp.zeros((ROWS_BF16, 1), F32).at[:H_B, 0].set(b_f.astype(F32))
    w_qg = w_b_qg[0].astype(BF16)
    w_o = w_b_out[0].astype(BF16)
    row = lambda v: v.reshape(1, d).astype(F32)
    g_mix0, g_mix1 = row(norm_mix[0]), row(norm_mix[1])
    g_ffn0, g_ffn1 = row(norm_ffn[0]), row(norm_ffn[1])
    g_kv, g_fin, gn = row(norm_kv), row(norm_final), row(gn_ret[0])
    q_scale = LOG2E * dh ** -0.5

    def trunk_tokens(h1, m_tile):
        bb, tt, _ = h1.shape
        h2 = _ffn_block(h1.reshape(bb * tt, d), g_ffn0, w_up, w_down, 0, tm=m_tile).reshape(bb, tt, d)
        return h2, _projections(h2, g_kv, g_mix1, w_kv, w_ft, b_col, w_qg, tm=m_tile,
                                q_scale=q_scale, c_scale=LOG2E)

    tc = _pick_tile(t, TOKEN_TILE)
    ret_tile = _pick_tile(t, RET_TILE)
    chunk = _pick_tile(ret_tile, RET_CHUNK)
    cos_p, sin_p = _rope_tables(jnp.arange(t), dk // 2)
    h1, s_prompt = _retention_layer(
        x_prompt, None, cos_p, sin_p, g_mix0, w_in, gn, w_out,
        _decay_tables(chunk, chunk, dk), perms, tc=ret_tile, chunk=chunk)
    h2, (k_p, v_p, kb_p, vb_p, lf_p, c_p, q_p, gate_p) = trunk_tokens(h1, tc)
    a_p = _prompt_attention(q_p, kb_p, vb_p, c_p, gate_p, tq=_pick_tile(t, ATTN_TILE),
                            heads=ATTN_HEADS)
    y_prompt = _ffn_block(h2.reshape(b * t, d), g_ffn1, w_up, w_down, 1, tm=tc,
                          attn=(a_p.reshape(b * t, d), w_o), g_final=g_fin).reshape(b, t, d)

    rows = SAMPLE_CHUNK
    cos_s, sin_s = _rope_tables(past + jnp.arange(rows), dk // 2)
    h1s, s_sample = _retention_layer(
        x_sample, state_ret[0], cos_s, sin_s, g_mix0, w_in, gn, w_out,
        _decay_tables(rows, td, dk), perms, tc=rows, chunk=rows, nseq=_pick_tile(bd, SAMPLE_SEQS))
    h1s = h1s.reshape(1, bd * td, d)
    h2s, (k_s, v_s, kb_s, vb_s, lf_s, _, q_s, gate_s) = trunk_tokens(h1s, bd * td)
    per_head = lambda a: a.reshape(bd * td * H_B, dh)
    lf_s_tok = jnp.swapaxes(lf_s[0], 0, 1)
    a_s = _sample_attention(
        page_table, per_head(q_s), per_head(gate_s), per_head(kb_s), per_head(vb_s),
        lf_s[0],
        cache_k.reshape(n_pool, page * H_B, dh), cache_v.reshape(n_pool, page * H_B, dh),
        jnp.swapaxes(cache_logf, 1, 2),
        group=_pick_tile(page_table.shape[1], PAGE_GROUP), t_new=td)
    y_sample = _ffn_block(h2s.reshape(bd * td, d), g_ffn1, w_up, w_down, 1, tm=bd * td,
                          attn=(a_s.reshape(bd * td, d), w_o), g_final=g_fin).reshape(bd, td, d)

    return (y_prompt, y_sample, s_prompt[None],
            k_p.reshape(b, t, H_B, dh), v_p.reshape(b, t, H_B, dh), jnp.swapaxes(lf_p, 1, 2),
            s_sample[None],
            k_s.reshape(bd, td, H_B, dh), v_s.reshape(bd, td, H_B, dh),
            lf_s_tok.reshape(bd, td, H_B))
```
